```python
import jax, jax.numpy as jnp
from jax import lax
import numpy as np

D_MODEL = 1024
BATCH = 8
SEQ = 2048
DEPTH = 4
DEC_BATCH = 2
DEC_SEQ = 16384
PAST_LEN = 128

N_MIXERS = 2
EXPAND = 2
D_INNER = EXPAND * D_MODEL
N_RWKV_LAYERS = (DEPTH + N_MIXERS - 1) // N_MIXERS
N_MLA_LAYERS = DEPTH // N_MIXERS

RWKV_HEAD = 64
RWKV_HEADS = D_INNER // RWKV_HEAD
DECAY_LORA = 64
ICL_LORA = 64
LN_X_EPS = 64e-5
RWKV_COL_STREAMS = ((0, D_INNER), (2, D_INNER), (3, D_INNER), (5, D_INNER),
                    (1, 2 * DECAY_LORA), (4, 2 * ICL_LORA))
RWKV_IN_COLS = 4 * D_INNER + 2 * DECAY_LORA + 2 * ICL_LORA

MLA_HEADS = 16
QK_NOPE = 128
QK_ROPE = 64
V_HEAD = 128
QK_HEAD = QK_NOPE + QK_ROPE
Q_LORA = 384
KV_LORA = 256
ROPE_THETA = 10000.0
Q_BLOCK = 128
MLA_IN_COLS = Q_LORA + KV_LORA + QK_ROPE + D_INNER

NORM_EPS = 1e-6

kernel_name = "rwkv7_mla_interleaved_bidir_encoder"


def rmsnorm(x, g):
    xf = x.astype(jnp.float32)
    y = xf * lax.rsqrt(jnp.mean(xf * xf, axis=-1, keepdims=True) + NORM_EPS) * g.astype(jnp.float32)
    return y.astype(x.dtype)


def _wkv_step(S, inp):
    r, w, k, v, kk, b = inp
    sa = jnp.einsum('bhvk,bhk->bhv', S, -kk)
    S = S * w[:, :, None, :] + sa[..., None] * b[:, :, None, :] + v[..., None] * k[:, :, None, :]
    return S, jnp.einsum('bhvk,bhk->bhv', S, r)


def rwkv_mixer(h, mu, w_in, w0, w2, a0, a2, k_k, k_a, r_k, lnx_g, lnx_b, w_out):
    B, T, _ = h.shape
    prev = jnp.pad(h[:, :-1], ((0, 0), (1, 0), (0, 0)))
    nxt = jnp.pad(h[:, 1:], ((0, 0), (0, 1), (0, 0)))
    xx = 0.5 * (prev + nxt) - h
    mu_cols = jnp.concatenate(
        [jnp.broadcast_to(mu[s][:, None], (D_MODEL, wd)) for s, wd in RWKV_COL_STREAMS], axis=1)
    w_eff = jnp.concatenate([w_in, mu_cols * w_in], axis=0)
    proj = (jnp.concatenate([h, xx], axis=-1) @ w_eff).astype(jnp.float32)
    E = D_INNER
    r, k, v, g, wl, al = jnp.split(proj, [E, 2 * E, 3 * E, 4 * E, 4 * E + 2 * DECAY_LORA], axis=-1)
    wl = wl.reshape(B, T, 2, DECAY_LORA)
    al = al.reshape(B, T, 2, ICL_LORA)
    w_pre = w0.astype(jnp.float32) + jnp.einsum('btdl,dle->btde', jnp.tanh(wl), w2.astype(jnp.float32))
    decay = jnp.exp(-jnp.exp(-jax.nn.softplus(-w_pre) - 0.5))
    a = jax.nn.sigmoid(a0.astype(jnp.float32) + jnp.einsum('btdl,dle->btde', al, a2.astype(jnp.float32)))
    kk = (k * k_k).reshape(B, T, RWKV_HEADS, RWKV_HEAD)
    kk = kk / jnp.maximum(jnp.sqrt(jnp.sum(kk * kk, axis=-1, keepdims=True)), 1e-12)
    kk = kk.reshape(B, T, E)
    k_dir = k[:, :, None, :] * (1.0 + (a - 1.0) * k_a)
    b_dir = kk[:, :, None, :] * a

    def tm(z):
        return z.reshape(B, T, RWKV_HEADS, RWKV_HEAD).transpose(1, 0, 2, 3)

    r_t, v_t, kk_t = tm(r), tm(v), tm(kk)
    S0 = jnp.zeros((B, RWKV_HEADS, RWKV_HEAD, RWKV_HEAD), jnp.float32)
    _, y_f = lax.scan(_wkv_step, S0, (r_t, tm(decay[:, :, 0]), tm(k_dir[:, :, 0]), v_t, kk_t, tm(b_dir[:, :, 0])))
    _, y_b = lax.scan(_wkv_step, S0, (r_t, tm(decay[:, :, 1]), tm(k_dir[:, :, 1]), v_t, kk_t, tm(b_dir[:, :, 1])),
                      reverse=True)
    y = (y_f + y_b).transpose(1, 0, 2, 3)
    mean = jnp.mean(y, axis=-1, keepdims=True)
    var = jnp.mean(jnp.square(y - mean), axis=-1, keepdims=True)
    y = ((y - mean) * lax.rsqrt(var + LN_X_EPS)).reshape(B, T, E) * lnx_g + lnx_b
    rk = (r[:, :, None, :] * k_dir).reshape(B, T, 2, RWKV_HEADS, RWKV_HEAD) * r_k.astype(jnp.float32)
    bonus = jnp.sum(rk, axis=(2, 4))[..., None] * v.reshape(B, T, RWKV_HEADS, RWKV_HEAD)
    out = (y + bonus.reshape(B, T, E)) * jax.nn.silu(g)
    return (out @ w_out).astype(h.dtype)


def _rope(x, cos, sin):
    xf = x.astype(jnp.float32)
    x1, x2 = jnp.split(xf, 2, axis=-1)
    return jnp.concatenate([x1 * cos - x2 * sin, x2 * cos + x1 * sin], axis=-1).astype(x.dtype)


def mla_mixer(h, w_in, q_norm_g, kv_norm_g, w_uq, w_ukv, w_out):
    B, T, _ = h.shape
    proj = h @ w_in
    c_q, c_kv, k_pe, g = jnp.split(proj, [Q_LORA, Q_LORA + KV_LORA, Q_LORA + KV_LORA + QK_ROPE], axis=-1)
    q = (rmsnorm(c_q, q_norm_g) @ w_uq).reshape(B, T, MLA_HEADS, QK_HEAD)
    kv = (rmsnorm(c_kv, kv_norm_g) @ w_ukv).reshape(B, T, MLA_HEADS, QK_NOPE + V_HEAD)
    q_nope, q_pe = jnp.split(q, [QK_NOPE], axis=-1)
    k_nope, v = jnp.split(kv, [QK_NOPE], axis=-1)
    inv_freq = 1.0 / (ROPE_THETA ** (jnp.arange(0, QK_ROPE, 2, dtype=jnp.float32) / QK_ROPE))
    ang = jnp.arange(T, dtype=jnp.float32)[:, None] * inv_freq[None, :]
    cos, sin = jnp.cos(ang)[:, None, :], jnp.sin(ang)[:, None, :]
    q_pe = _rope(q_pe, cos, sin)
    k_pe = _rope(k_pe[:, :, None, :], cos, sin)
    q = jnp.concatenate([q_nope, q_pe], axis=-1) * (QK_HEAD ** -0.5)
    k = jnp.concatenate([k_nope, jnp.broadcast_to(k_pe, (B, T, MLA_HEADS, QK_ROPE))], axis=-1)
    nb = T // Q_BLOCK
    qb = q.reshape(B, nb, Q_BLOCK, MLA_HEADS, QK_HEAD).transpose(1, 0, 2, 3, 4)

    def attend(q_blk):
        s = jnp.einsum('bqhd,bkhd->bhqk', q_blk, k, preferred_element_type=jnp.float32)
        p = jax.nn.softmax(s, axis=-1).astype(v.dtype)
        return jnp.einsum('bhqk,bkhd->bqhd', p, v)

    o = lax.map(attend, qb)
    o = o.transpose(1, 0, 2, 3, 4).reshape(B, T, MLA_HEADS * V_HEAD)
    return ((o * jax.nn.silu(g)) @ w_out).astype(h.dtype)


def trunk(x, ln_g, final_g, rw_mu, rw_in, rw_w0, rw_w2, rw_a0, rw_a2, rw_kk, rw_ka, rw_rk,
          rw_lnx_g, rw_lnx_b, rw_out, ml_in, ml_qn, ml_kvn, ml_uq, ml_ukv, ml_out):
    for i in range(DEPTH):
        h = rmsnorm(x, ln_g[i])
        j = i // N_MIXERS
        if i % N_MIXERS == 0:
            x = x + rwkv_mixer(h, rw_mu[j], rw_in[j], rw_w0[j], rw_w2[j], rw_a0[j], rw_a2[j],
                               rw_kk[j], rw_ka[j], rw_rk[j], rw_lnx_g[j], rw_lnx_b[j], rw_out[j])
        else:
            x = x + mla_mixer(h, ml_in[j], ml_qn[j], ml_kvn[j], ml_uq[j], ml_ukv[j], ml_out[j])
    return rmsnorm(x, final_g)


def setup_inputs(seed: int = 0) -> dict:
    key = jax.random.key(seed)
    ks = jax.random.split(key, 24)
    f32 = jnp.float32

    def nrm(k, shape, s):
        return jax.random.normal(k, shape, f32) * s

    NA, NB, E = N_RWKV_LAYERS, N_MLA_LAYERS, D_INNER
    return {
        "x_prompt": nrm(ks[0], (BATCH, SEQ, D_MODEL), 1.0),
        "x_sample": nrm(ks[1], (DEC_BATCH, DEC_SEQ, D_MODEL), 1.0),
        "ln_g": 1.0 + nrm(ks[2], (DEPTH, D_MODEL), 0.02),
        "final_g": 1.0 + nrm(ks[3], (D_MODEL,), 0.02),
        "rw_mu": jax.random.uniform(ks[4], (NA, 6, D_MODEL), f32),
        "rw_in": nrm(ks[5], (NA, D_MODEL, RWKV_IN_COLS), D_MODEL ** -0.5),
        "rw_w0": jax.random.uniform(ks[6], (NA, 2, E), f32, minval=-6.5, maxval=-1.0),
        "rw_w2": nrm(ks[7], (NA, 2, DECAY_LORA, E), 0.1 * DECAY_LORA ** -0.5),
        "rw_a0": nrm(ks[8], (NA, 2, E), 0.1),
        "rw_a2": nrm(ks[9], (NA, 2, ICL_LORA, E), 0.5 * ICL_LORA ** -0.5),
        "rw_kk": 0.85 + nrm(ks[10], (NA, E), 0.05),
        "rw_ka": 1.0 + nrm(ks[11], (NA, E), 0.05),
        "rw_rk": nrm(ks[12], (NA, RWKV_HEADS, RWKV_HEAD), 0.05),
        "rw_lnx_g": 1.0 + nrm(ks[13], (NA, E), 0.02),
        "rw_lnx_b": nrm(ks[14], (NA, E), 0.02),
        "rw_out": nrm(ks[15], (NA, E, D_MODEL), E ** -0.5),
        "ml_in": nrm(ks[16], (NB, D_MODEL, MLA_IN_COLS), D_MODEL ** -0.5),
        "ml_qn": 1.0 + nrm(ks[17], (NB, Q_LORA), 0.02),
        "ml_kvn": 1.0 + nrm(ks[18], (NB, KV_LORA), 0.02),
        "ml_uq": nrm(ks[19], (NB, Q_LORA, MLA_HEADS * QK_HEAD), Q_LORA ** -0.5),
        "ml_ukv": nrm(ks[20], (NB, KV_LORA, MLA_HEADS * (QK_NOPE + V_HEAD)), KV_LORA ** -0.5),
        "ml_out": nrm(ks[21], (NB, E, D_MODEL), E ** -0.5),
    }


def reference(x_prompt, x_sample, ln_g, final_g, rw_mu, rw_in, rw_w0, rw_w2, rw_a0, rw_a2,
              rw_kk, rw_ka, rw_rk, rw_lnx_g, rw_lnx_b, rw_out, ml_in, ml_qn, ml_kvn, ml_uq,
              ml_ukv, ml_out):
    y_prompt = trunk(x_prompt, ln_g, final_g, rw_mu, rw_in, rw_w0, rw_w2, rw_a0, rw_a2, rw_kk, rw_ka,
                     rw_rk, rw_lnx_g, rw_lnx_b, rw_out, ml_in, ml_qn, ml_kvn, ml_uq, ml_ukv, ml_out)
    y_sample = trunk(x_sample, ln_g, final_g, rw_mu, rw_in, rw_w0, rw_w2, rw_a0, rw_a2, rw_kk, rw_ka,
                     rw_rk, rw_lnx_g, rw_lnx_b, rw_out, ml_in, ml_qn, ml_kvn, ml_uq, ml_ukv, ml_out)
    return (y_prompt, y_sample)
```

```python
import functools

import jax
import jax.numpy as jnp
from jax import lax
from jax.experimental import pallas as pl
from jax.experimental.pallas import tpu as pltpu

F32 = jnp.float32
BF16 = jnp.bfloat16

D_MODEL = 1024
D_INNER = 2048
DEPTH = 4
RWKV_HEAD = 64
RWKV_HEADS = D_INNER // RWKV_HEAD
LORA = 64
LN_X_EPS = 64e-5
NORM_EPS = 1e-6
MLA_HEADS = 16
QK_NOPE = 128
QK_ROPE = 64
V_HEAD = 128
QK_HEAD = QK_NOPE + QK_ROPE
Q_LORA = 384
KV_LORA = 256
ROPE_THETA = 10000.0
Q_PAD = 256
SMALL_COLS = Q_LORA + KV_LORA + 2 * QK_ROPE

LANES = 128
SUBLANES = 8
WKV_CHUNK = 64
WKV_HEADS_PER_STEP = 4
INV_BASE = 16
EXP_NEG_HALF = 0.6065306597126334
VMEM_LIMIT = 48 * 1024 * 1024

HIGHEST = lax.Precision.HIGHEST


def _bdot(a, b):
    return jnp.dot(a.astype(BF16), b.astype(BF16), preferred_element_type=F32)


def _bdot_nt(a, b):
    return lax.dot_general(a.astype(BF16), b.astype(BF16), (((1,), (1,)), ((), ())),
                           preferred_element_type=F32)


def _bdot_tn(a, b):
    return lax.dot_general(a.astype(BF16), b.astype(BF16), (((0,), (0,)), ((), ())),
                           preferred_element_type=F32)


def _split2(x):
    hi = x.astype(BF16)
    lo = (x - hi.astype(F32)).astype(BF16)
    return hi, lo


def _dot_x2(x, m_bf16):
    hi, lo = _split2(x)
    return (jnp.dot(hi, m_bf16, preferred_element_type=F32)
            + jnp.dot(lo, m_bf16, preferred_element_type=F32))


def _sigmoid(x):
    return 1.0 / (1.0 + jnp.exp(-x))


def _params(sem):
    return pltpu.CompilerParams(dimension_semantics=sem, vmem_limit_bytes=VMEM_LIMIT)


def _rms(x, g):
    return x * lax.rsqrt(jnp.mean(x * x, axis=-1, keepdims=True) + NORM_EPS) * g


def _rw_prep_kernel(x_ref, xp_ref, xn_ref, g_ref, mu_ref, xs4_ref, xs2_ref, *, tm, seq):
    i = pl.program_id(0)
    g = g_ref[...]
    h = _rms(x_ref[...], g)
    hp = _rms(xp_ref[SUBLANES - 1:SUBLANES, :], g)
    hn = _rms(xn_ref[0:1, :], g)
    row0 = i * tm
    hp = jnp.where((row0 % seq) == 0, 0.0, hp)
    hn = jnp.where(((row0 + tm) % seq) == 0, 0.0, hn)
    rid = lax.broadcasted_iota(jnp.int32, (tm, 1), 0)
    prev = jnp.where(rid == 0, hp, pltpu.roll(h, 1, axis=0))
    nxt = jnp.where(rid == tm - 1, hn, pltpu.roll(h, tm - 1, axis=0))
    xx = 0.5 * (prev + nxt) - h
    mu = mu_ref[...]
    for n, s in enumerate((0, 2, 3, 5)):
        xs4_ref[n] = (h + mu[s:s + 1, :] * xx).astype(BF16)
    xs2_ref[:, :D_MODEL] = (h + mu[1:2, :] * xx).astype(BF16)
    xs2_ref[:, D_MODEL:] = (h + mu[4:5, :] * xx).astype(BF16)


def _rw_prep(x, ln_g, mu, seq, tm=256):
    n = x.shape[0]
    nb8 = n // SUBLANES
    per = tm // SUBLANES
    return pl.pallas_call(
        functools.partial(_rw_prep_kernel, tm=tm, seq=seq),
        grid=(n // tm,),
        in_specs=[
            pl.BlockSpec((tm, D_MODEL), lambda i: (i, 0)),
            pl.BlockSpec((SUBLANES, D_MODEL), lambda i: (jnp.maximum(i * per - 1, 0), 0)),
            pl.BlockSpec((SUBLANES, D_MODEL), lambda i: (jnp.minimum((i + 1) * per, nb8 - 1), 0)),
            pl.BlockSpec((1, D_MODEL), lambda i: (0, 0)),
            pl.BlockSpec((6, D_MODEL), lambda i: (0, 0)),
        ],
        out_specs=[
            pl.BlockSpec((4, tm, D_MODEL), lambda i: (0, i, 0)),
            pl.BlockSpec((tm, 2 * D_MODEL), lambda i: (i, 0)),
        ],
        out_shape=[
            jax.ShapeDtypeStruct((4, n, D_MODEL), BF16),
            jax.ShapeDtypeStruct((n, 2 * D_MODEL), BF16),
        ],
        compiler_params=_params(("parallel",)),
        name="rw_prep",
    )(x, x, x, ln_g, mu)


def _norm_kernel(x_ref, g_ref, o_ref):
    o_ref[...] = _rms(x_ref[...], g_ref[...]).astype(o_ref.dtype)


def _rmsnorm_bf16(x, g, tm=512):
    n = x.shape[0]
    return pl.pallas_call(
        _norm_kernel,
        grid=(n // tm,),
        in_specs=[pl.BlockSpec((tm, D_MODEL), lambda i: (i, 0)),
                  pl.BlockSpec((1, D_MODEL), lambda i: (0, 0))],
        out_specs=pl.BlockSpec((tm, D_MODEL), lambda i: (i, 0)),
        out_shape=jax.ShapeDtypeStruct((n, D_MODEL), BF16),
        compiler_params=_params(("parallel",)),
        name="rmsnorm",
    )(x, g)


def _mm_kernel(x_ref, w_ref, o_ref):
    o_ref[...] = jnp.dot(x_ref[...], w_ref[...], preferred_element_type=F32).astype(o_ref.dtype)


def _matmul(x, w, tm=512, tn=512):
    n, k = x.shape
    m = w.shape[1]
    tn = min(tn, m)
    return pl.pallas_call(
        _mm_kernel,
        grid=(n // tm, m // tn),
        in_specs=[pl.BlockSpec((tm, k), lambda i, j: (i, 0)),
                  pl.BlockSpec((k, tn), lambda i, j: (0, j))],
        out_specs=pl.BlockSpec((tm, tn), lambda i, j: (i, j)),
        out_shape=jax.ShapeDtypeStruct((n, m), F32),
        compiler_params=_params(("parallel", "parallel")),
        name="matmul",
    )(x, w)


def _matmul_streams(xs, w, tm=512, tn=512):
    ns, n, k = xs.shape
    per = D_INNER // tn
    return pl.pallas_call(
        _mm_kernel,
        grid=(n // tm, ns, per),
        in_specs=[pl.BlockSpec((None, tm, k), lambda i, s, j: (s, i, 0)),
                  pl.BlockSpec((k, tn), lambda i, s, j: (0, s * per + j))],
        out_specs=pl.BlockSpec((tm, tn), lambda i, s, j: (i, s * per + j)),
        out_shape=jax.ShapeDtypeStruct((n, ns * D_INNER), F32),
        compiler_params=_params(("parallel", "parallel", "parallel")),
        name="matmul_streams",
    )(xs, w)


def _unit_tri_inverse(a, row, col):
    c = a.shape[0]
    eye = (row == col).astype(F32)
    same = (row // INV_BASE) == (col // INV_BASE)
    d = jnp.where(same, a, 0.0)
    x = eye + d
    p = d
    span = 2
    while span < INV_BASE:
        p = _bdot(p, p)
        x = x + _bdot(x, p)
        span *= 2
    blk = INV_BASE
    while blk < c:
        pair = ((row // (2 * blk)) == (col // (2 * blk))) & jnp.logical_not(
            (row // blk) == (col // blk))
        off = jnp.where(pair, a, 0.0)
        x = x + _bdot(x, _bdot(off, x))
        blk *= 2
    return x


def _wkv_kernel(rf_ref, kf_ref, vf_ref, sf_ref, rb_ref, kb_ref, vb_ref, sb_ref,
                w0_ref, w2_ref, a0_ref, a2_ref, kk_ref, ka_ref,
                yf_ref, yb_ref, state_ref, *, chunk, heads):
    c = chunk
    n = RWKV_HEAD
    width = heads * n

    @pl.when(pl.program_id(2) == 0)
    def _():
        state_ref[...] = jnp.zeros_like(state_ref)

    row = lax.broadcasted_iota(jnp.int32, (c, c), 0)
    col = lax.broadcasted_iota(jnp.int32, (c, c), 1)
    lrow = lax.broadcasted_iota(jnp.int32, (width, width), 0)
    lcol = lax.broadcasted_iota(jnp.int32, (width, width), 1)
    seg_ones = ((lrow // n) == (lcol // n)).astype(BF16)

    dirs = ((rf_ref, kf_ref, vf_ref, sf_ref, yf_ref), (rb_ref, kb_ref, vb_ref, sb_ref, yb_ref))
    for d, (r_ref, k_ref, v_ref, s_ref, y_ref) in enumerate(dirs):
        strict = (row > col) if d == 0 else (row < col)
        incl = (row >= col) if d == 0 else (row <= col)
        r = r_ref[...]
        k = k_ref[...]
        v = v_ref[...]
        sm = s_ref[...]
        wl = sm[:, d * LORA:(d + 1) * LORA]
        al = sm[:, (2 + d) * LORA:(3 + d) * LORA]
        w_pre = w0_ref[d:d + 1, :] + _bdot(jnp.tanh(wl), w2_ref[d])
        logw = -EXP_NEG_HALF * _sigmoid(w_pre)
        a = _sigmoid(a0_ref[d:d + 1, :] + _bdot(al, a2_ref[d]))
        kk = k * kk_ref[...]
        ss = _dot_x2(kk * kk, seg_ones)
        kk = kk / jnp.maximum(jnp.sqrt(ss), 1e-12)
        kdir = k * (1.0 + (a - 1.0) * ka_ref[...])
        b = kk * a
        tri = incl.astype(BF16)
        l_hi = logw.astype(BF16)
        l_r = logw - l_hi.astype(F32)
        l_mid = l_r.astype(BF16)
        l_lo = (l_r - l_mid.astype(F32)).astype(BF16)
        cl = (jnp.dot(tri, l_hi, preferred_element_type=F32)
              + jnp.dot(tri, l_mid, preferred_element_type=F32)
              + jnp.dot(tri, l_lo, preferred_element_type=F32))
        cl_end = cl[c - 1:c, :] if d == 0 else cl[0:1, :]
        e_in = jnp.exp(cl)
        e_ex = jnp.exp(cl - logw)
        e_neg = jnp.exp(-cl)
        e_end = jnp.exp(cl_end - cl)
        ones_c = jnp.ones((c, LANES), BF16)
        tn = (((0,), (0,)), ((), ()))
        w_end_t = jnp.exp(lax.dot_general(l_hi, ones_c, tn, preferred_element_type=F32)
                          + lax.dot_general(l_mid, ones_c, tn, preferred_element_type=F32)
                          + lax.dot_general(l_lo, ones_c, tn, preferred_element_type=F32))
        rt = r * e_in
        at = -kk * e_ex
        bt = b * e_neg
        kt = kdir * e_neg
        bh = b * e_end
        kh = kdir * e_end
        ys = []
        for h in range(heads):
            sl = slice(h * n, (h + 1) * n)
            g = _bdot_nt(jnp.concatenate([at[:, sl], rt[:, sl]], axis=0),
                         jnp.concatenate([bt[:, sl], kt[:, sl]], axis=0))
            a_ab = jnp.where(strict, g[:c, :c], 0.0)
            a_ak = jnp.where(strict, g[:c, c:], 0.0)
            a_rb = jnp.where(incl, g[c:, :c], 0.0)
            a_rk = jnp.where(incl, g[c:, c:], 0.0)
            tinv = _unit_tri_inverse(a_ab, row, col)
            vh = v[:, sl]
            q = _bdot(tinv, jnp.concatenate([at[:, sl], _bdot(a_ak, vh)], axis=1))
            z = jnp.concatenate(
                [q, jnp.concatenate([jnp.zeros((c, n), F32), vh], axis=1)], axis=0)
            ry = _bdot(jnp.concatenate([a_rb, a_rk], axis=1), z)
            ms = _bdot_tn(jnp.concatenate([bh[:, sl], kh[:, sl]], axis=0), z)
            s_old = state_ref[d, h]
            r_eff = rt[:, sl] + ry[:, :n]
            ys.append(_bdot(r_eff, s_old) + ry[:, n:])
            state_ref[d, h] = (_bdot(ms[:, :n], s_old) + w_end_t[sl, :n] * s_old + ms[:, n:])
        y_ref[...] = jnp.concatenate(ys, axis=1)


def _wkv(proj, small, w0, w2, a0, a2, k_k, k_a, batch, seq):
    n_tok = proj.shape[0]
    c = WKV_CHUNK
    hg = WKV_HEADS_PER_STEP
    width = hg * RWKV_HEAD
    nc = seq // c
    groups = D_INNER // width

    def fwd(s):
        return lambda b, g, t: (b * nc + t, s * groups + g)

    def bwd(s):
        return lambda b, g, t: (b * nc + (nc - 1 - t), s * groups + g)

    blk = lambda f: pl.BlockSpec((c, width), f)
    par = lambda shape: pl.BlockSpec(shape, lambda b, g, t: (0,) * (len(shape) - 1) + (g,))
    return pl.pallas_call(
        functools.partial(_wkv_kernel, chunk=c, heads=hg),
        grid=(batch, groups, nc),
        in_specs=[
            blk(fwd(0)), blk(fwd(1)), blk(fwd(2)),
            pl.BlockSpec((c, 4 * LORA), lambda b, g, t: (b * nc + t, 0)),
            blk(bwd(0)), blk(bwd(1)), blk(bwd(2)),
            pl.BlockSpec((c, 4 * LORA), lambda b, g, t: (b * nc + (nc - 1 - t), 0)),
            par((2, width)), par((2, LORA, width)), par((2, width)), par((2, LORA, width)),
            par((1, width)), par((1, width)),
        ],
        out_specs=[
            pl.BlockSpec((c, width), lambda b, g, t: (b * nc + t, g)),
            pl.BlockSpec((c, width), lambda b, g, t: (b * nc + (nc - 1 - t), g)),
        ],
        out_shape=[jax.ShapeDtypeStruct((n_tok, D_INNER), F32),
                   jax.ShapeDtypeStruct((n_tok, D_INNER), F32)],
        scratch_shapes=[pltpu.VMEM((2, hg, RWKV_HEAD, RWKV_HEAD), F32)],
        compiler_params=_params(("parallel", "parallel", "arbitrary")),
        name="wkv7_chunked",
    )(proj, proj, proj, small, proj, proj, proj, small, w0, w2, a0, a2, k_k, k_a)


def _rw_post_kernel(yf_ref, yb_ref, r_ref, k_ref, v_ref, g_ref, s_ref, x_ref,
                    a0_ref, a2_ref, ka_ref, rk_ref, lg_ref, lb_ref, ind_ref, indt_ref, wo_ref,
                    o_ref):
    ind = ind_ref[...]
    indt = indt_ref[...]

    def seg_sum(z):
        return _dot_x2(_dot_x2(z, ind), indt)

    y = yf_ref[...] + yb_ref[...]
    mean = seg_sum(y) * (1.0 / RWKV_HEAD)
    yc = y - mean
    var = seg_sum(yc * yc) * (1.0 / RWKV_HEAD)
    yn = yc * lax.rsqrt(var + LN_X_EPS) * lg_ref[...] + lb_ref[...]
    sm = s_ref[...]
    a_sum = (_sigmoid(a0_ref[0:1, :] + _bdot(sm[:, 2 * LORA:3 * LORA], a2_ref[0]))
             + _sigmoid(a0_ref[1:2, :] + _bdot(sm[:, 3 * LORA:4 * LORA], a2_ref[1])))
    k_sum = k_ref[...] * (2.0 + (a_sum - 2.0) * ka_ref[...])
    bonus = seg_sum(r_ref[...] * k_sum * rk_ref[...]) * v_ref[...]
    g = g_ref[...]
    out = (yn + bonus) * (g * _sigmoid(g))
    o_ref[...] = x_ref[...] + _bdot(out, wo_ref[...])


def _rw_post(yf, yb, proj, small, x, a0, a2, k_a, r_k, lnx_g, lnx_b, ind, indt, w_out, tm=128):
    n = x.shape[0]
    e = D_INNER
    row = lambda w, j: pl.BlockSpec((tm, w), lambda i: (i, j))
    full = lambda shape: pl.BlockSpec(shape, lambda i: (0,) * len(shape))
    return pl.pallas_call(
        _rw_post_kernel,
        grid=(n // tm,),
        in_specs=[
            row(e, 0), row(e, 0),
            row(e, 0), row(e, 1), row(e, 2), row(e, 3),
            row(4 * LORA, 0), row(D_MODEL, 0),
            full((2, e)), full((2, LORA, e)), full((1, e)), full((1, e)), full((1, e)),
            full((1, e)), full((e, LANES)), full((LANES, e)), full((e, D_MODEL)),
        ],
        out_specs=row(D_MODEL, 0),
        out_shape=jax.ShapeDtypeStruct((n, D_MODEL), F32),
        compiler_params=_params(("parallel",)),
        name="rw_post",
    )(yf, yb, proj, proj, proj, proj, small, x, a0, a2, k_a, r_k, lnx_g, lnx_b, ind, indt, w_out)


def _ml_up_kernel(s_ref, cs_ref, sn_ref, qn_ref, kvn_ref, wq_ref, wkv_ref,
                  q_ref, kn_ref, v_ref, kpe_ref):
    sm = s_ref[...]
    cs = cs_ref[...]
    sn = sn_ref[...]
    scale = QK_HEAD ** -0.5

    def rms(x, g):
        return x * lax.rsqrt(jnp.mean(x * x, axis=-1, keepdims=True) + NORM_EPS) * g

    def rope(x2):
        return x2 * cs + pltpu.roll(x2, QK_ROPE, axis=1) * sn

    cq = rms(sm[:, :Q_LORA], qn_ref[...])
    ckv = rms(sm[:, Q_LORA:Q_LORA + KV_LORA], kvn_ref[...])
    q = _bdot(cq, wq_ref[...])
    kv = _bdot(ckv, wkv_ref[...])
    for h in range(MLA_HEADS):
        o = h * Q_PAD
        q_ref[:, o:o + QK_NOPE] = (q[:, o:o + QK_NOPE] * scale).astype(BF16)
        q_ref[:, o + QK_NOPE:o + Q_PAD] = (rope(q[:, o + QK_NOPE:o + Q_PAD]) * scale).astype(BF16)
    kn_ref[...] = kv[:, :MLA_HEADS * QK_NOPE].astype(BF16)
    v_ref[...] = kv[:, MLA_HEADS * QK_NOPE:].astype(BF16)
    kpe_ref[...] = rope(sm[:, Q_LORA + KV_LORA:]).astype(BF16)


def _ml_up(small, cs, sn, qn, kvn, wq, wkv, seq, tm=256):
    n = small.shape[0]
    per_seq = seq // tm
    row = lambda w: pl.BlockSpec((tm, w), lambda i: (i, 0))
    pos = pl.BlockSpec((tm, LANES), lambda i: (i % per_seq, 0))
    full = lambda shape: pl.BlockSpec(shape, lambda i: (0,) * len(shape))
    hq = MLA_HEADS * Q_PAD
    hv = MLA_HEADS * V_HEAD
    return pl.pallas_call(
        _ml_up_kernel,
        grid=(n // tm,),
        in_specs=[row(SMALL_COLS), pos, pos, full((1, Q_LORA)), full((1, KV_LORA)),
                  full((Q_LORA, hq)), full((KV_LORA, 2 * hv))],
        out_specs=[row(hq), row(hv), row(hv), row(LANES)],
        out_shape=[jax.ShapeDtypeStruct((n, hq), BF16), jax.ShapeDtypeStruct((n, hv), BF16),
                   jax.ShapeDtypeStruct((n, hv), BF16), jax.ShapeDtypeStruct((n, LANES), BF16)],
        compiler_params=_params(("parallel",)),
        name="mla_up",
    )(small, cs, sn, qn, kvn, wq, wkv)


def _attn_kernel(q_ref, kn_ref, kpe_ref, v_ref, o_ref, m_ref, l_ref, acc_ref):
    ki = pl.program_id(3)

    @pl.when(ki == 0)
    def _():
        m_ref[...] = jnp.full_like(m_ref, -jnp.inf)
        l_ref[...] = jnp.zeros_like(l_ref)
        acc_ref[...] = jnp.zeros_like(acc_ref)

    k = jnp.concatenate([kn_ref[...], kpe_ref[...]], axis=1)
    s = lax.dot_general(q_ref[...], k, (((1,), (1,)), ((), ())), preferred_element_type=F32)
    m_old = m_ref[...]
    m_new = jnp.maximum(m_old, jnp.max(s, axis=1, keepdims=True))
    alpha = jnp.exp(m_old - m_new)
    p = jnp.exp(s - m_new)
    l_ref[...] = alpha * l_ref[...] + jnp.sum(p, axis=1, keepdims=True)
    acc_ref[...] = alpha * acc_ref[...] + jnp.dot(p.astype(BF16), v_ref[...],
                                                  preferred_element_type=F32)
    m_ref[...] = m_new

    @pl.when(ki == pl.num_programs(3) - 1)
    def _():
        o_ref[...] = (acc_ref[...] / l_ref[...]).astype(o_ref.dtype)


def _attention(q, kn, kpe, v, batch, seq, tq=512, tk=1024):
    n = q.shape[0]
    nq = seq // tq
    nk = seq // tk
    return pl.pallas_call(
        _attn_kernel,
        grid=(batch, MLA_HEADS, nq, nk),
        in_specs=[
            pl.BlockSpec((tq, Q_PAD), lambda b, h, i, j: (b * nq + i, h)),
            pl.BlockSpec((tk, QK_NOPE), lambda b, h, i, j: (b * nk + j, h)),
            pl.BlockSpec((tk, LANES), lambda b, h, i, j: (b * nk + j, 0)),
            pl.BlockSpec((tk, V_HEAD), lambda b, h, i, j: (b * nk + j, h)),
        ],
        out_specs=pl.BlockSpec((tq, V_HEAD), lambda b, h, i, j: (b * nq + i, h)),
        out_shape=jax.ShapeDtypeStruct((n, MLA_HEADS * V_HEAD), BF16),
        scratch_shapes=[pltpu.VMEM((tq, 1), F32), pltpu.VMEM((tq, 1), F32),
                        pltpu.VMEM((tq, V_HEAD), F32)],
        compiler_params=_params(("parallel", "parallel", "parallel", "arbitrary")),
        name="mla_flash_attention",
    )(q, kn, kpe, v)


def _ml_post_kernel(o_ref, g_ref, x_ref, wo_ref, fg_ref, y_ref, *, final):
    g = g_ref[...]
    out = o_ref[...].astype(F32) * (g * _sigmoid(g))
    y = x_ref[...] + _bdot(out, wo_ref[...])
    if final:
        y = _rms(y, fg_ref[...])
    y_ref[...] = y


def _ml_post(o, g, x, w_out, final_g, final, tm=256):
    n = x.shape[0]
    row = lambda w: pl.BlockSpec((tm, w), lambda i: (i, 0))
    full = lambda shape: pl.BlockSpec(shape, lambda i: (0,) * len(shape))
    return pl.pallas_call(
        functools.partial(_ml_post_kernel, final=final),
        grid=(n // tm,),
        in_specs=[row(D_INNER), row(D_INNER), row(D_MODEL), full((D_INNER, D_MODEL)),
                  full((1, D_MODEL))],
        out_specs=row(D_MODEL),
        out_shape=jax.ShapeDtypeStruct((n, D_MODEL), F32),
        compiler_params=_params(("parallel",)),
        name="mla_post",
    )(o, g, x, w_out, final_g)


def _rotate_half_cols(w):
    half = QK_ROPE // 2
    return jnp.concatenate([-w[..., half:], w[..., :half]], axis=-1)


def _prep_rwkv(j, rw_in, rw_w0, rw_w2, rw_a0, rw_a2, rw_kk, rw_ka, rw_rk, rw_lnx_g, rw_lnx_b,
               rw_out):
    e = D_INNER
    w = rw_in[j]
    w_main = w[:, :4 * e].astype(BF16)
    ww = w[:, 4 * e:4 * e + 2 * LORA]
    wa = w[:, 4 * e + 2 * LORA:]
    z = jnp.zeros_like(ww)
    w_small = jnp.concatenate([jnp.concatenate([ww, z], axis=1),
                               jnp.concatenate([z, wa], axis=1)], axis=0).astype(BF16)
    return dict(
        w_main=w_main, w_small=w_small, w0=rw_w0[j], w2=rw_w2[j].astype(BF16), a0=rw_a0[j],
        a2=rw_a2[j].astype(BF16), k_k=rw_kk[j][None], k_a=rw_ka[j][None],
        r_k=rw_rk[j].reshape(1, e), lnx_g=rw_lnx_g[j][None], lnx_b=rw_lnx_b[j][None],
        w_out=rw_out[j].astype(BF16))


def _prep_mla(j, ml_in, ml_qn, ml_kvn, ml_uq, ml_ukv, ml_out):
    w = ml_in[j]
    lo = Q_LORA + KV_LORA
    w_kpe = w[:, lo:lo + QK_ROPE]
    w_small = jnp.concatenate([w[:, :lo], w_kpe, _rotate_half_cols(w_kpe)], axis=1).astype(BF16)
    w_g = w[:, lo + QK_ROPE:].astype(BF16)
    uq = ml_uq[j].reshape(Q_LORA, MLA_HEADS, QK_HEAD)
    pe = uq[..., QK_NOPE:]
    wq = jnp.concatenate([uq[..., :QK_NOPE], pe, _rotate_half_cols(pe)], axis=-1)
    wq = wq.reshape(Q_LORA, MLA_HEADS * Q_PAD).astype(BF16)
    ukv = ml_ukv[j].reshape(KV_LORA, MLA_HEADS, QK_NOPE + V_HEAD)
    wkv = jnp.concatenate([ukv[..., :QK_NOPE].reshape(KV_LORA, -1),
                           ukv[..., QK_NOPE:].reshape(KV_LORA, -1)], axis=1).astype(BF16)
    return dict(w_small=w_small, w_g=w_g, qn=ml_qn[j][None], kvn=ml_kvn[j][None], wq=wq,
                wkv=wkv, w_out=ml_out[j].astype(BF16))


def _rope_tables(seq):
    inv_freq = 1.0 / (ROPE_THETA ** (jnp.arange(0, QK_ROPE, 2, dtype=F32) / QK_ROPE))
    ang = jnp.arange(seq, dtype=F32)[:, None] * inv_freq[None, :]
    z = jnp.zeros((seq, LANES - QK_ROPE), F32)
    cs = jnp.concatenate([jnp.cos(ang), jnp.cos(ang), z], axis=1)
    sn = jnp.concatenate([jnp.sin(ang), jnp.sin(ang), z], axis=1)
    return cs, sn


def _head_indicator():
    e = lax.broadcasted_iota(jnp.int32, (D_INNER, LANES), 0)
    h = lax.broadcasted_iota(jnp.int32, (D_INNER, LANES), 1)
    ind = ((e // RWKV_HEAD) == h).astype(BF16)
    return ind, ind.T


def _trunk(x3, ln_g, final_g, rw, ml):
    batch, seq, _ = x3.shape
    x = x3.reshape(batch * seq, D_MODEL)
    ind, indt = _head_indicator()
    cs, sn = _rope_tables(seq)
    for i in range(DEPTH):
        g = ln_g[i][None]
        if i % 2 == 0:
            p = rw[i // 2]
            xs4, xs2 = _rw_prep(x, g, p["mu"], seq)
            proj = _matmul_streams(xs4, p["w_main"])
            small = _matmul(xs2, p["w_small"])
            yf, yb = _wkv(proj, small, p["w0"], p["w2"], p["a0"], p["a2"], p["k_k"], p["k_a"],
                          batch, seq)
            x = _rw_post(yf, yb, proj, small, x, p["a0"], p["a2"], p["k_a"], p["r_k"],
                         p["lnx_g"], p["lnx_b"], ind, indt, p["w_out"])
        else:
            p = ml[i // 2]
            h = _rmsnorm_bf16(x, g)
            small = _matmul(h, p["w_small"], tn=SMALL_COLS)
            gate = _matmul(h, p["w_g"])
            q, kn, v, kpe = _ml_up(small, cs, sn, p["qn"], p["kvn"], p["wq"], p["wkv"], seq)
            o = _attention(q, kn, kpe, v, batch, seq)
            x = _ml_post(o, gate, x, p["w_out"], final_g[None], final=(i == DEPTH - 1))
    return x.reshape(batch, seq, D_MODEL)


def kernel(x_prompt, x_sample, ln_g, final_g, rw_mu, rw_in, rw_w0, rw_w2, rw_a0, rw_a2, rw_kk,
           rw_ka, rw_rk, rw_lnx_g, rw_lnx_b, rw_out, ml_in, ml_qn, ml_kvn, ml_uq, ml_ukv, ml_out):
    rw = []
    for j in range(rw_in.shape[0]):
        p = _prep_rwkv(j, rw_in, rw_w0, rw_w2, rw_a0, rw_a2, rw_kk, rw_ka, rw_rk, rw_lnx_g,
                       rw_lnx_b, rw_out)
        p["mu"] = rw_mu[j]
        rw.append(p)
    ml = [_prep_mla(j, ml_in, ml_qn, ml_kvn, ml_uq, ml_ukv, ml_out) for j in range(ml_in.shape[0])]
    y_prompt = _trunk(x_prompt, ln_g, final_g, rw, ml)
    y_sample = _trunk(x_sample, ln_g, final_g, rw, ml)
    return (y_prompt, y_sample)
```

```python
import functools

import jax
import jax.numpy as jnp
from jax import lax
from jax.experimental import pallas as pl
from jax.experimental.pallas import tpu as pltpu

F32 = jnp.float32
BF16 = jnp.bfloat16

D_MODEL = 1024
D_INNER = 2048
DEPTH = 4
RWKV_HEAD = 64
RWKV_HEADS = D_INNER // RWKV_HEAD
LORA = 64
LN_X_EPS = 64e-5
NORM_EPS = 1e-6
MLA_HEADS = 16
QK_NOPE = 128
QK_ROPE = 64
V_HEAD = 128
QK_HEAD = QK_NOPE + QK_ROPE
Q_LORA = 384
KV_LORA = 256
ROPE_THETA = 10000.0
Q_PAD = 256
SMALL_COLS = Q_LORA + KV_LORA + 2 * QK_ROPE

LANES = 128
SUBLANES = 8
WKV_CHUNK = 64
WKV_HEADS_PER_STEP = 8
INV_BASE = 16
EXP_NEG_HALF = 0.6065306597126334
VMEM_LIMIT = 48 * 1024 * 1024

HIGHEST = lax.Precision.HIGHEST


def _bdot(a, b):
    return jnp.dot(a.astype(BF16), b.astype(BF16), preferred_element_type=F32)


def _bdot_nt(a, b):
    return lax.dot_general(a.astype(BF16), b.astype(BF16), (((1,), (1,)), ((), ())),
                           preferred_element_type=F32)


def _bdot_tn(a, b):
    return lax.dot_general(a.astype(BF16), b.astype(BF16), (((0,), (0,)), ((), ())),
                           preferred_element_type=F32)


def _split2(x):
    hi = x.astype(BF16)
    lo = (x - hi.astype(F32)).astype(BF16)
    return hi, lo


def _dot_x2(x, m_bf16):
    hi, lo = _split2(x)
    return (jnp.dot(hi, m_bf16, preferred_element_type=F32)
            + jnp.dot(lo, m_bf16, preferred_element_type=F32))


def _sigmoid(x):
    return 1.0 / (1.0 + jnp.exp(-x))


def _params(sem):
    return pltpu.CompilerParams(dimension_semantics=sem, vmem_limit_bytes=VMEM_LIMIT)


def _rms(x, g):
    return x * lax.rsqrt(jnp.mean(x * x, axis=-1, keepdims=True) + NORM_EPS) * g


def _rw_prep_kernel(x_ref, xp_ref, xn_ref, g_ref, mu_ref, xs4_ref, xs2_ref, *, tm, seq):
    i = pl.program_id(0)
    g = g_ref[...]
    h = _rms(x_ref[...], g)
    hp = _rms(xp_ref[SUBLANES - 1:SUBLANES, :], g)
    hn = _rms(xn_ref[0:1, :], g)
    row0 = i * tm
    hp = jnp.where((row0 % seq) == 0, 0.0, hp)
    hn = jnp.where(((row0 + tm) % seq) == 0, 0.0, hn)
    rid = lax.broadcasted_iota(jnp.int32, (tm, 1), 0)
    prev = jnp.where(rid == 0, hp, pltpu.roll(h, 1, axis=0))
    nxt = jnp.where(rid == tm - 1, hn, pltpu.roll(h, tm - 1, axis=0))
    xx = 0.5 * (prev + nxt) - h
    mu = mu_ref[...]
    for n, s in enumerate((0, 2, 3, 5)):
        xs4_ref[n] = (h + mu[s:s + 1, :] * xx).astype(BF16)
    xs2_ref[:, :D_MODEL] = (h + mu[1:2, :] * xx).astype(BF16)
    xs2_ref[:, D_MODEL:] = (h + mu[4:5, :] * xx).astype(BF16)


def _rw_prep(x, ln_g, mu, seq, tm=256):
    n = x.shape[0]
    nb8 = n // SUBLANES
    per = tm // SUBLANES
    return pl.pallas_call(
        functools.partial(_rw_prep_kernel, tm=tm, seq=seq),
        grid=(n // tm,),
        in_specs=[
            pl.BlockSpec((tm, D_MODEL), lambda i: (i, 0)),
            pl.BlockSpec((SUBLANES, D_MODEL), lambda i: (jnp.maximum(i * per - 1, 0), 0)),
            pl.BlockSpec((SUBLANES, D_MODEL), lambda i: (jnp.minimum((i + 1) * per, nb8 - 1), 0)),
            pl.BlockSpec((1, D_MODEL), lambda i: (0, 0)),
            pl.BlockSpec((6, D_MODEL), lambda i: (0, 0)),
        ],
        out_specs=[
            pl.BlockSpec((4, tm, D_MODEL), lambda i: (0, i, 0)),
            pl.BlockSpec((tm, 2 * D_MODEL), lambda i: (i, 0)),
        ],
        out_shape=[
            jax.ShapeDtypeStruct((4, n, D_MODEL), BF16),
            jax.ShapeDtypeStruct((n, 2 * D_MODEL), BF16),
        ],
        compiler_params=_params(("parallel",)),
        name="rw_prep",
    )(x, x, x, ln_g, mu)


def _norm_kernel(x_ref, g_ref, o_ref):
    o_ref[...] = _rms(x_ref[...], g_ref[...]).astype(o_ref.dtype)


def _rmsnorm_bf16(x, g, tm=512):
    n = x.shape[0]
    return pl.pallas_call(
        _norm_kernel,
        grid=(n // tm,),
        in_specs=[pl.BlockSpec((tm, D_MODEL), lambda i: (i, 0)),
                  pl.BlockSpec((1, D_MODEL), lambda i: (0, 0))],
        out_specs=pl.BlockSpec((tm, D_MODEL), lambda i: (i, 0)),
        out_shape=jax.ShapeDtypeStruct((n, D_MODEL), BF16),
        compiler_params=_params(("parallel",)),
        name="rmsnorm",
    )(x, g)


def _mm_kernel(x_ref, w_ref, o_ref):
    o_ref[...] = jnp.dot(x_ref[...], w_ref[...], preferred_element_type=F32).astype(o_ref.dtype)


def _matmul(x, w, tm=512, tn=512):
    n, k = x.shape
    m = w.shape[1]
    tn = min(tn, m)
    return pl.pallas_call(
        _mm_kernel,
        grid=(n // tm, m // tn),
        in_specs=[pl.BlockSpec((tm, k), lambda i, j: (i, 0)),
                  pl.BlockSpec((k, tn), lambda i, j: (0, j))],
        out_specs=pl.BlockSpec((tm, tn), lambda i, j: (i, j)),
        out_shape=jax.ShapeDtypeStruct((n, m), F32),
        compiler_params=_params(("parallel", "parallel")),
        name="matmul",
    )(x, w)


def _matmul_streams(xs, w, tm=512, tn=512):
    ns, n, k = xs.shape
    per = D_INNER // tn
    return pl.pallas_call(
        _mm_kernel,
        grid=(n // tm, ns, per),
        in_specs=[pl.BlockSpec((None, tm, k), lambda i, s, j: (s, i, 0)),
                  pl.BlockSpec((k, tn), lambda i, s, j: (0, s * per + j))],
        out_specs=pl.BlockSpec((tm, tn), lambda i, s, j: (i, s * per + j)),
        out_shape=jax.ShapeDtypeStruct((n, ns * D_INNER), F32),
        compiler_params=_params(("parallel", "parallel", "parallel")),
        name="matmul_streams",
    )(xs, w)


def _unit_tri_inverses(mats, row, col):
    c = mats[0].shape[0]
    eye = (row == col).astype(F32)
    same = (row // INV_BASE) == (col // INV_BASE)
    ps = [jnp.where(same, a, 0.0) for a in mats]
    xs = [eye + p for p in ps]
    span = 2
    while span < INV_BASE:
        ps = [_bdot(p, p) for p in ps]
        xs = [x + _bdot(x, p) for x, p in zip(xs, ps)]
        span *= 2
    blk = INV_BASE
    while blk < c:
        pair = ((row // (2 * blk)) == (col // (2 * blk))) & jnp.logical_not(
            (row // blk) == (col // blk))
        ts = [_bdot(jnp.where(pair, a, 0.0), x) for a, x in zip(mats, xs)]
        xs = [x + _bdot(x, t) for x, t in zip(xs, ts)]
        blk *= 2
    return xs


def _wkv_kernel(rf_ref, kf_ref, vf_ref, sf_ref, rb_ref, kb_ref, vb_ref, sb_ref,
                w0_ref, w2_ref, a0_ref, a2_ref, kk_ref, ka_ref,
                yf_ref, yb_ref, state_ref, *, chunk, heads):
    c = chunk
    n = RWKV_HEAD
    width = heads * n

    @pl.when(pl.program_id(2) == 0)
    def _():
        state_ref[...] = jnp.zeros_like(state_ref)

    row = lax.broadcasted_iota(jnp.int32, (c, c), 0)
    col = lax.broadcasted_iota(jnp.int32, (c, c), 1)
    lrow = lax.broadcasted_iota(jnp.int32, (width, width), 0)
    lcol = lax.broadcasted_iota(jnp.int32, (width, width), 1)
    seg_ones = ((lrow // n) == (lcol // n)).astype(BF16)

    units = []
    dirs = ((rf_ref, kf_ref, vf_ref, sf_ref), (rb_ref, kb_ref, vb_ref, sb_ref))
    for d, (r_ref, k_ref, v_ref, s_ref) in enumerate(dirs):
        strict = (row > col) if d == 0 else (row < col)
        incl = (row >= col) if d == 0 else (row <= col)
        r = r_ref[...]
        k = k_ref[...]
        v = v_ref[...]
        sm = s_ref[...]
        wl = sm[:, d * LORA:(d + 1) * LORA]
        al = sm[:, (2 + d) * LORA:(3 + d) * LORA]
        w_pre = w0_ref[d:d + 1, :] + _bdot(jnp.tanh(wl), w2_ref[d])
        logw = -EXP_NEG_HALF * _sigmoid(w_pre)
        a = _sigmoid(a0_ref[d:d + 1, :] + _bdot(al, a2_ref[d]))
        kk = k * kk_ref[...]
        ss = _dot_x2(kk * kk, seg_ones)
        kk = kk / jnp.maximum(jnp.sqrt(ss), 1e-12)
        kdir = k * (1.0 + (a - 1.0) * ka_ref[...])
        b = kk * a
        tri = incl.astype(BF16)
        l_hi = logw.astype(BF16)
        l_r = logw - l_hi.astype(F32)
        l_mid = l_r.astype(BF16)
        l_lo = (l_r - l_mid.astype(F32)).astype(BF16)
        cl = (jnp.dot(tri, l_hi, preferred_element_type=F32)
              + jnp.dot(tri, l_mid, preferred_element_type=F32)
              + jnp.dot(tri, l_lo, preferred_element_type=F32))
        cl_end = cl[c - 1:c, :] if d == 0 else cl[0:1, :]
        e_in = jnp.exp(cl)
        e_ex = jnp.exp(cl - logw)
        e_neg = jnp.exp(-cl)
        e_end = jnp.exp(cl_end - cl)
        ones_c = jnp.ones((c, LANES), BF16)
        tn = (((0,), (0,)), ((), ()))
        w_end_t = jnp.exp(lax.dot_general(l_hi, ones_c, tn, preferred_element_type=F32)
                          + lax.dot_general(l_mid, ones_c, tn, preferred_element_type=F32)
                          + lax.dot_general(l_lo, ones_c, tn, preferred_element_type=F32))
        rt = (r * e_in).astype(BF16)
        at = (-kk * e_ex).astype(BF16)
        bt = (b * e_neg).astype(BF16)
        kt = (kdir * e_neg).astype(BF16)
        bh = (b * e_end).astype(BF16)
        kh = (kdir * e_end).astype(BF16)
        vb = v.astype(BF16)
        for h in range(heads):
            sl = slice(h * n, (h + 1) * n)
            units.append(dict(
                d=d, h=h, strict=strict, incl=incl, rt=rt[:, sl], at=at[:, sl], v=vb[:, sl],
                lhs=jnp.concatenate([at[:, sl], rt[:, sl]], axis=0),
                rhs=jnp.concatenate([bt[:, sl], kt[:, sl]], axis=0),
                end=jnp.concatenate([bh[:, sl], kh[:, sl]], axis=0),
                w_end=w_end_t[sl, :n]))

    gs = [_bdot_nt(u["lhs"], u["rhs"]) for u in units]
    a_ab = [jnp.where(u["strict"], g[:c, :c], 0.0) for u, g in zip(units, gs)]
    a_ak = [jnp.where(u["strict"], g[:c, c:], 0.0).astype(BF16) for u, g in zip(units, gs)]
    a_r = [jnp.concatenate([jnp.where(u["incl"], g[c:, :c], 0.0),
                            jnp.where(u["incl"], g[c:, c:], 0.0)], axis=1).astype(BF16)
           for u, g in zip(units, gs)]
    pv = [_bdot(a, u["v"]) for a, u in zip(a_ak, units)]
    tinv = _unit_tri_inverses(a_ab, row, col)
    qs = [_bdot(t, jnp.concatenate([u["at"], p.astype(BF16)], axis=1))
          for t, u, p in zip(tinv, units, pv)]
    zs = [jnp.concatenate([q.astype(BF16),
                           jnp.concatenate([jnp.zeros((c, n), BF16), u["v"]], axis=1)], axis=0)
          for q, u in zip(qs, units)]
    rys = [_bdot(a, z) for a, z in zip(a_r, zs)]
    mss = [_bdot_tn(u["end"], z) for u, z in zip(units, zs)]
    s_old = [state_ref[u["d"], u["h"]] for u in units]
    ys = [_bdot(u["rt"].astype(F32) + ry[:, :n], s) + ry[:, n:]
          for u, ry, s in zip(units, rys, s_old)]
    for u, ms, s in zip(units, mss, s_old):
        state_ref[u["d"], u["h"]] = _bdot(ms[:, :n], s) + u["w_end"] * s + ms[:, n:]
    yf_ref[...] = jnp.concatenate(ys[:heads], axis=1)
    yb_ref[...] = jnp.concatenate(ys[heads:], axis=1)


def _wkv(proj, small, w0, w2, a0, a2, k_k, k_a, batch, seq):
    n_tok = proj.shape[0]
    c = WKV_CHUNK
    hg = WKV_HEADS_PER_STEP
    width = hg * RWKV_HEAD
    nc = seq // c
    groups = D_INNER // width

    def fwd(s):
        return lambda b, g, t: (b * nc + t, s * groups + g)

    def bwd(s):
        return lambda b, g, t: (b * nc + (nc - 1 - t), s * groups + g)

    blk = lambda f: pl.BlockSpec((c, width), f)
    par = lambda shape: pl.BlockSpec(shape, lambda b, g, t: (0,) * (len(shape) - 1) + (g,))
    return pl.pallas_call(
        functools.partial(_wkv_kernel, chunk=c, heads=hg),
        grid=(batch, groups, nc),
        in_specs=[
            blk(fwd(0)), blk(fwd(1)), blk(fwd(2)),
            pl.BlockSpec((c, 4 * LORA), lambda b, g, t: (b * nc + t, 0)),
            blk(bwd(0)), blk(bwd(1)), blk(bwd(2)),
            pl.BlockSpec((c, 4 * LORA), lambda b, g, t: (b * nc + (nc - 1 - t), 0)),
            par((2, width)), par((2, LORA, width)), par((2, width)), par((2, LORA, width)),
            par((1, width)), par((1, width)),
        ],
        out_specs=[
            pl.BlockSpec((c, width), lambda b, g, t: (b * nc + t, g)),
            pl.BlockSpec((c, width), lambda b, g, t: (b * nc + (nc - 1 - t), g)),
        ],
        out_shape=[jax.ShapeDtypeStruct((n_tok, D_INNER), F32),
                   jax.ShapeDtypeStruct((n_tok, D_INNER), F32)],
        scratch_shapes=[pltpu.VMEM((2, hg, RWKV_HEAD, RWKV_HEAD), F32)],
        compiler_params=_params(("parallel", "parallel", "arbitrary")),
        name="wkv7_chunked",
    )(proj, proj, proj, small, proj, proj, proj, small, w0, w2, a0, a2, k_k, k_a)


def _rw_post_kernel(yf_ref, yb_ref, r_ref, k_ref, v_ref, g_ref, s_ref, x_ref,
                    a0_ref, a2_ref, ka_ref, rk_ref, lg_ref, lb_ref, ind_ref, indt_ref, wo_ref,
                    o_ref):
    ind = ind_ref[...]
    indt = indt_ref[...]

    def seg_sum(z):
        return _dot_x2(_dot_x2(z, ind), indt)

    y = yf_ref[...] + yb_ref[...]
    mean = seg_sum(y) * (1.0 / RWKV_HEAD)
    yc = y - mean
    var = seg_sum(yc * yc) * (1.0 / RWKV_HEAD)
    yn = yc * lax.rsqrt(var + LN_X_EPS) * lg_ref[...] + lb_ref[...]
    sm = s_ref[...]
    a_sum = (_sigmoid(a0_ref[0:1, :] + _bdot(sm[:, 2 * LORA:3 * LORA], a2_ref[0]))
             + _sigmoid(a0_ref[1:2, :] + _bdot(sm[:, 3 * LORA:4 * LORA], a2_ref[1])))
    k_sum = k_ref[...] * (2.0 + (a_sum - 2.0) * ka_ref[...])
    bonus = seg_sum(r_ref[...] * k_sum * rk_ref[...]) * v_ref[...]
    g = g_ref[...]
    out = (yn + bonus) * (g * _sigmoid(g))
    o_ref[...] = x_ref[...] + _bdot(out, wo_ref[...])


def _rw_post(yf, yb, proj, small, x, a0, a2, k_a, r_k, lnx_g, lnx_b, ind, indt, w_out, tm=128):
    n = x.shape[0]
    e = D_INNER
    row = lambda w, j: pl.BlockSpec((tm, w), lambda i: (i, j))
    full = lambda shape: pl.BlockSpec(shape, lambda i: (0,) * len(shape))
    return pl.pallas_call(
        _rw_post_kernel,
        grid=(n // tm,),
        in_specs=[
            row(e, 0), row(e, 0),
            row(e, 0), row(e, 1), row(e, 2), row(e, 3),
            row(4 * LORA, 0), row(D_MODEL, 0),
            full((2, e)), full((2, LORA, e)), full((1, e)), full((1, e)), full((1, e)),
            full((1, e)), full((e, LANES)), full((LANES, e)), full((e, D_MODEL)),
        ],
        out_specs=row(D_MODEL, 0),
        out_shape=jax.ShapeDtypeStruct((n, D_MODEL), F32),
        compiler_params=_params(("parallel",)),
        name="rw_post",
    )(yf, yb, proj, proj, proj, proj, small, x, a0, a2, k_a, r_k, lnx_g, lnx_b, ind, indt, w_out)


def _ml_up_kernel(s_ref, cs_ref, sn_ref, qn_ref, kvn_ref, wq_ref, wkv_ref,
                  q_ref, kn_ref, v_ref, kpe_ref):
    sm = s_ref[...]
    cs = cs_ref[...]
    sn = sn_ref[...]
    scale = QK_HEAD ** -0.5

    def rms(x, g):
        return x * lax.rsqrt(jnp.mean(x * x, axis=-1, keepdims=True) + NORM_EPS) * g

    def rope(x2):
        return x2 * cs + pltpu.roll(x2, QK_ROPE, axis=1) * sn

    cq = rms(sm[:, :Q_LORA], qn_ref[...])
    ckv = rms(sm[:, Q_LORA:Q_LORA + KV_LORA], kvn_ref[...])
    q = _bdot(cq, wq_ref[...])
    kv = _bdot(ckv, wkv_ref[...])
    for h in range(MLA_HEADS):
        o = h * Q_PAD
        q_ref[:, o:o + QK_NOPE] = (q[:, o:o + QK_NOPE] * scale).astype(BF16)
        q_ref[:, o + QK_NOPE:o + Q_PAD] = (rope(q[:, o + QK_NOPE:o + Q_PAD]) * scale).astype(BF16)
    kn_ref[...] = kv[:, :MLA_HEADS * QK_NOPE].astype(BF16)
    v_ref[...] = kv[:, MLA_HEADS * QK_NOPE:].astype(BF16)
    kpe_ref[...] = rope(sm[:, Q_LORA + KV_LORA:]).astype(BF16)


def _ml_up(small, cs, sn, qn, kvn, wq, wkv, seq, tm=256):
    n = small.shape[0]
    per_seq = seq // tm
    row = lambda w: pl.BlockSpec((tm, w), lambda i: (i, 0))
    pos = pl.BlockSpec((tm, LANES), lambda i: (i % per_seq, 0))
    full = lambda shape: pl.BlockSpec(shape, lambda i: (0,) * len(shape))
    hq = MLA_HEADS * Q_PAD
    hv = MLA_HEADS * V_HEAD
    return pl.pallas_call(
        _ml_up_kernel,
        grid=(n // tm,),
        in_specs=[row(SMALL_COLS), pos, pos, full((1, Q_LORA)), full((1, KV_LORA)),
                  full((Q_LORA, hq)), full((KV_LORA, 2 * hv))],
        out_specs=[row(hq), row(hv), row(hv), row(LANES)],
        out_shape=[jax.ShapeDtypeStruct((n, hq), BF16), jax.ShapeDtypeStruct((n, hv), BF16),
                   jax.ShapeDtypeStruct((n, hv), BF16), jax.ShapeDtypeStruct((n, LANES), BF16)],
        compiler_params=_params(("parallel",)),
        name="mla_up",
    )(small, cs, sn, qn, kvn, wq, wkv)


def _attn_kernel(q_ref, kn_ref, kpe_ref, v_ref, o_ref, m_ref, l_ref, acc_ref):
    ki = pl.program_id(3)

    @pl.when(ki == 0)
    def _():
        m_ref[...] = jnp.full_like(m_ref, -jnp.inf)
        l_ref[...] = jnp.zeros_like(l_ref)
        acc_ref[...] = jnp.zeros_like(acc_ref)

    k = jnp.concatenate([kn_ref[...], kpe_ref[...]], axis=1)
    s = lax.dot_general(q_ref[...], k, (((1,), (1,)), ((), ())), preferred_element_type=F32)
    m_old = m_ref[...]
    m_new = jnp.maximum(m_old, jnp.max(s, axis=1, keepdims=True))
    alpha = jnp.exp(m_old - m_new)
    p = jnp.exp(s - m_new)
    l_ref[...] = alpha * l_ref[...] + jnp.sum(p, axis=1, keepdims=True)
    acc_ref[...] = alpha * acc_ref[...] + jnp.dot(p.astype(BF16), v_ref[...],
                                                  preferred_element_type=F32)
    m_ref[...] = m_new

    @pl.when(ki == pl.num_programs(3) - 1)
    def _():
        o_ref[...] = (acc_ref[...] / l_ref[...]).astype(o_ref.dtype)


def _attention(q, kn, kpe, v, batch, seq, tq=512, tk=1024):
    n = q.shape[0]
    nq = seq // tq
    nk = seq // tk
    return pl.pallas_call(
        _attn_kernel,
        grid=(batch, MLA_HEADS, nq, nk),
        in_specs=[
            pl.BlockSpec((tq, Q_PAD), lambda b, h, i, j: (b * nq + i, h)),
            pl.BlockSpec((tk, QK_NOPE), lambda b, h, i, j: (b * nk + j, h)),
            pl.BlockSpec((tk, LANES), lambda b, h, i, j: (b * nk + j, 0)),
            pl.BlockSpec((tk, V_HEAD), lambda b, h, i, j: (b * nk + j, h)),
        ],
        out_specs=pl.BlockSpec((tq, V_HEAD), lambda b, h, i, j: (b * nq + i, h)),
        out_shape=jax.ShapeDtypeStruct((n, MLA_HEADS * V_HEAD), BF16),
        scratch_shapes=[pltpu.VMEM((tq, 1), F32), pltpu.VMEM((tq, 1), F32),
                        pltpu.VMEM((tq, V_HEAD), F32)],
        compiler_params=_params(("parallel", "parallel", "parallel", "arbitrary")),
        name="mla_flash_attention",
    )(q, kn, kpe, v)


def _ml_post_kernel(o_ref, g_ref, x_ref, wo_ref, fg_ref, y_ref, *, final):
    g = g_ref[...]
    out = o_ref[...].astype(F32) * (g * _sigmoid(g))
    y = x_ref[...] + _bdot(out, wo_ref[...])
    if final:
        y = _rms(y, fg_ref[...])
    y_ref[...] = y


def _ml_post(o, g, x, w_out, final_g, final, tm=256):
    n = x.shape[0]
    row = lambda w: pl.BlockSpec((tm, w), lambda i: (i, 0))
    full = lambda shape: pl.BlockSpec(shape, lambda i: (0,) * len(shape))
    return pl.pallas_call(
        functools.partial(_ml_post_kernel, final=final),
        grid=(n // tm,),
        in_specs=[row(D_INNER), row(D_INNER), row(D_MODEL), full((D_INNER, D_MODEL)),
                  full((1, D_MODEL))],
        out_specs=row(D_MODEL),
        out_shape=jax.ShapeDtypeStruct((n, D_MODEL), F32),
        compiler_params=_params(("parallel",)),
        name="mla_post",
    )(o, g, x, w_out, final_g)


def _rotate_half_cols(w):
    half = QK_ROPE // 2
    return jnp.concatenate([-w[..., half:], w[..., :half]], axis=-1)


def _prep_rwkv(j, rw_in, rw_w0, rw_w2, rw_a0, rw_a2, rw_kk, rw_ka, rw_rk, rw_lnx_g, rw_lnx_b,
               rw_out):
    e = D_INNER
    w = rw_in[j]
    w_main = w[:, :4 * e].astype(BF16)
    ww = w[:, 4 * e:4 * e + 2 * LORA]
    wa = w[:, 4 * e + 2 * LORA:]
    z = jnp.zeros_like(ww)
    w_small = jnp.concatenate([jnp.concatenate([ww, z], axis=1),
                               jnp.concatenate([z, wa], axis=1)], axis=0).astype(BF16)
    return dict(
        w_main=w_main, w_small=w_small, w0=rw_w0[j], w2=rw_w2[j].astype(BF16), a0=rw_a0[j],
        a2=rw_a2[j].astype(BF16), k_k=rw_kk[j][None], k_a=rw_ka[j][None],
        r_k=rw_rk[j].reshape(1, e), lnx_g=rw_lnx_g[j][None], lnx_b=rw_lnx_b[j][None],
        w_out=rw_out[j].astype(BF16))


def _prep_mla(j, ml_in, ml_qn, ml_kvn, ml_uq, ml_ukv, ml_out):
    w = ml_in[j]
    lo = Q_LORA + KV_LORA
    w_kpe = w[:, lo:lo + QK_ROPE]
    w_small = jnp.concatenate([w[:, :lo], w_kpe, _rotate_half_cols(w_kpe)], axis=1).astype(BF16)
    w_g = w[:, lo + QK_ROPE:].astype(BF16)
    uq = ml_uq[j].reshape(Q_LORA, MLA_HEADS, QK_HEAD)
    pe = uq[..., QK_NOPE:]
    wq = jnp.concatenate([uq[..., :QK_NOPE], pe, _rotate_half_cols(pe)], axis=-1)
    wq = wq.reshape(Q_LORA, MLA_HEADS * Q_PAD).astype(BF16)
    ukv = ml_ukv[j].reshape(KV_LORA, MLA_HEADS, QK_NOPE + V_HEAD)
    wkv = jnp.concatenate([ukv[..., :QK_NOPE].reshape(KV_LORA, -1),
                           ukv[..., QK_NOPE:].reshape(KV_LORA, -1)], axis=1).astype(BF16)
    return dict(w_small=w_small, w_g=w_g, qn=ml_qn[j][None], kvn=ml_kvn[j][None], wq=wq,
                wkv=wkv, w_out=ml_out[j].astype(BF16))


def _rope_tables(seq):
    inv_freq = 1.0 / (ROPE_THETA ** (jnp.arange(0, QK_ROPE, 2, dtype=F32) / QK_ROPE))
    ang = jnp.arange(seq, dtype=F32)[:, None] * inv_freq[None, :]
    z = jnp.zeros((seq, LANES - QK_ROPE), F32)
    cs = jnp.concatenate([jnp.cos(ang), jnp.cos(ang), z], axis=1)
    sn = jnp.concatenate([jnp.sin(ang), jnp.sin(ang), z], axis=1)
    return cs, sn


def _head_indicator():
    e = lax.broadcasted_iota(jnp.int32, (D_INNER, LANES), 0)
    h = lax.broadcasted_iota(jnp.int32, (D_INNER, LANES), 1)
    ind = ((e // RWKV_HEAD) == h).astype(BF16)
    return ind, ind.T


def _trunk(x3, ln_g, final_g, rw, ml):
    batch, seq, _ = x3.shape
    x = x3.reshape(batch * seq, D_MODEL)
    ind, indt = _head_indicator()
    cs, sn = _rope_tables(seq)
    for i in range(DEPTH):
        g = ln_g[i][None]
        if i % 2 == 0:
            p = rw[i // 2]
            xs4, xs2 = _rw_prep(x, g, p["mu"], seq)
            proj = _matmul_streams(xs4, p["w_main"])
            small = _matmul(xs2, p["w_small"])
            yf, yb = _wkv(proj, small, p["w0"], p["w2"], p["a0"], p["a2"], p["k_k"], p["k_a"],
                          batch, seq)
            x = _rw_post(yf, yb, proj, small, x, p["a0"], p["a2"], p["k_a"], p["r_k"],
                         p["lnx_g"], p["lnx_b"], ind, indt, p["w_out"])
        else:
            p = ml[i // 2]
            h = _rmsnorm_bf16(x, g)
            small = _matmul(h, p["w_small"], tn=SMALL_COLS)
            gate = _matmul(h, p["w_g"])
            q, kn, v, kpe = _ml_up(small, cs, sn, p["qn"], p["kvn"], p["wq"], p["wkv"], seq)
            o = _attention(q, kn, kpe, v, batch, seq)
            x = _ml_post(o, gate, x, p["w_out"], final_g[None], final=(i == DEPTH - 1))
    return x.reshape(batch, seq, D_MODEL)


def kernel(x_prompt, x_sample, ln_g, final_g, rw_mu, rw_in, rw_w0, rw_w2, rw_a0, rw_a2, rw_kk,
           rw_ka, rw_rk, rw_lnx_g, rw_lnx_b, rw_out, ml_in, ml_qn, ml_kvn, ml_uq, ml_ukv, ml_out):
    rw = []
    for j in range(rw_in.shape[0]):
        p = _prep_rwkv(j, rw_in, rw_w0, rw_w2, rw_a0, rw_a2, rw_kk, rw_ka, rw_rk, rw_lnx_g,
                       rw_lnx_b, rw_out)
        p["mu"] = rw_mu[j]
        rw.append(p)
    ml = [_prep_mla(j, ml_in, ml_qn, ml_kvn, ml_uq, ml_ukv, ml_out) for j in range(ml_in.shape[0])]
    y_prompt = _trunk(x_prompt, ln_g, final_g, rw, ml)
    y_sample = _trunk(x_sample, ln_g, final_g, rw, ml)
    return (y_prompt, y_sample)
```

```python
import functools

import jax
import jax.numpy as jnp
from jax import lax
from jax.experimental import pallas as pl
from jax.experimental.pallas import tpu as pltpu

F32 = jnp.float32
BF16 = jnp.bfloat16

D_MODEL = 1024
D_INNER = 2048
DEPTH = 4
RWKV_HEAD = 64
RWKV_HEADS = D_INNER // RWKV_HEAD
LORA = 64
LN_X_EPS = 64e-5
NORM_EPS = 1e-6
MLA_HEADS = 16
QK_NOPE = 128
QK_ROPE = 64
V_HEAD = 128
QK_HEAD = QK_NOPE + QK_ROPE
Q_LORA = 384
KV_LORA = 256
ROPE_THETA = 10000.0
Q_PAD = 256
SMALL_COLS = Q_LORA + KV_LORA + 2 * QK_ROPE

LANES = 128
SUBLANES = 8
WKV_CHUNK = 64
WKV_HEADS_PER_STEP = 8
INV_BASE = 16
EXP_NEG_HALF = 0.6065306597126334
LOG2_E = 1.4426950408889634
MAX_JUMP = 100.0
VMEM_LIMIT = 48 * 1024 * 1024

HIGHEST = lax.Precision.HIGHEST


def _bdot(a, b):
    return jnp.dot(a.astype(BF16), b.astype(BF16), preferred_element_type=F32)


def _bdot_nt(a, b):
    return lax.dot_general(a.astype(BF16), b.astype(BF16), (((1,), (1,)), ((), ())),
                           preferred_element_type=F32)


def _bdot_tn(a, b):
    return lax.dot_general(a.astype(BF16), b.astype(BF16), (((0,), (0,)), ((), ())),
                           preferred_element_type=F32)


def _split2(x):
    hi = x.astype(BF16)
    lo = (x - hi.astype(F32)).astype(BF16)
    return hi, lo


def _dot_x2(x, m_bf16):
    hi, lo = _split2(x)
    return (jnp.dot(hi, m_bf16, preferred_element_type=F32)
            + jnp.dot(lo, m_bf16, preferred_element_type=F32))


def _sigmoid(x):
    return 1.0 / (1.0 + jnp.exp(-x))


def _params(sem):
    return pltpu.CompilerParams(dimension_semantics=sem, vmem_limit_bytes=VMEM_LIMIT)


def _rms(x, g):
    return x * lax.rsqrt(jnp.mean(x * x, axis=-1, keepdims=True) + NORM_EPS) * g


def _rw_prep_kernel(x_ref, xp_ref, xn_ref, g_ref, mu_ref, xs4_ref, xs2_ref, *, tm, seq):
    i = pl.program_id(0)
    g = g_ref[...]
    h = _rms(x_ref[...], g)
    hp = _rms(xp_ref[SUBLANES - 1:SUBLANES, :], g)
    hn = _rms(xn_ref[0:1, :], g)
    row0 = i * tm
    hp = jnp.where((row0 % seq) == 0, 0.0, hp)
    hn = jnp.where(((row0 + tm) % seq) == 0, 0.0, hn)
    rid = lax.broadcasted_iota(jnp.int32, (tm, 1), 0)
    prev = jnp.where(rid == 0, hp, pltpu.roll(h, 1, axis=0))
    nxt = jnp.where(rid == tm - 1, hn, pltpu.roll(h, tm - 1, axis=0))
    xx = 0.5 * (prev + nxt) - h
    mu = mu_ref[...]
    for n, s in enumerate((0, 2, 3, 5)):
        xs4_ref[n] = (h + mu[s:s + 1, :] * xx).astype(BF16)
    xs2_ref[:, :D_MODEL] = (h + mu[1:2, :] * xx).astype(BF16)
    xs2_ref[:, D_MODEL:] = (h + mu[4:5, :] * xx).astype(BF16)


def _rw_prep(x, ln_g, mu, seq, tm=256):
    n = x.shape[0]
    nb8 = n // SUBLANES
    per = tm // SUBLANES
    return pl.pallas_call(
        functools.partial(_rw_prep_kernel, tm=tm, seq=seq),
        grid=(n // tm,),
        in_specs=[
            pl.BlockSpec((tm, D_MODEL), lambda i: (i, 0)),
            pl.BlockSpec((SUBLANES, D_MODEL), lambda i: (jnp.maximum(i * per - 1, 0), 0)),
            pl.BlockSpec((SUBLANES, D_MODEL), lambda i: (jnp.minimum((i + 1) * per, nb8 - 1), 0)),
            pl.BlockSpec((1, D_MODEL), lambda i: (0, 0)),
            pl.BlockSpec((6, D_MODEL), lambda i: (0, 0)),
        ],
        out_specs=[
            pl.BlockSpec((4, tm, D_MODEL), lambda i: (0, i, 0)),
            pl.BlockSpec((tm, 2 * D_MODEL), lambda i: (i, 0)),
        ],
        out_shape=[
            jax.ShapeDtypeStruct((4, n, D_MODEL), BF16),
            jax.ShapeDtypeStruct((n, 2 * D_MODEL), BF16),
        ],
        compiler_params=_params(("parallel",)),
        name="rw_prep",
    )(x, x, x, ln_g, mu)


def _norm_kernel(x_ref, g_ref, o_ref):
    o_ref[...] = _rms(x_ref[...], g_ref[...]).astype(o_ref.dtype)


def _rmsnorm_bf16(x, g, tm=512):
    n = x.shape[0]
    return pl.pallas_call(
        _norm_kernel,
        grid=(n // tm,),
        in_specs=[pl.BlockSpec((tm, D_MODEL), lambda i: (i, 0)),
                  pl.BlockSpec((1, D_MODEL), lambda i: (0, 0))],
        out_specs=pl.BlockSpec((tm, D_MODEL), lambda i: (i, 0)),
        out_shape=jax.ShapeDtypeStruct((n, D_MODEL), BF16),
        compiler_params=_params(("parallel",)),
        name="rmsnorm",
    )(x, g)


def _mm_kernel(x_ref, w_ref, o_ref):
    o_ref[...] = jnp.dot(x_ref[...], w_ref[...], preferred_element_type=F32).astype(o_ref.dtype)


def _matmul(x, w, tm=512, tn=512):
    n, k = x.shape
    m = w.shape[1]
    tn = min(tn, m)
    return pl.pallas_call(
        _mm_kernel,
        grid=(n // tm, m // tn),
        in_specs=[pl.BlockSpec((tm, k), lambda i, j: (i, 0)),
                  pl.BlockSpec((k, tn), lambda i, j: (0, j))],
        out_specs=pl.BlockSpec((tm, tn), lambda i, j: (i, j)),
        out_shape=jax.ShapeDtypeStruct((n, m), F32),
        compiler_params=_params(("parallel", "parallel")),
        name="matmul",
    )(x, w)


def _matmul_streams(xs, w, tm=512, tn=512):
    ns, n, k = xs.shape
    per = D_INNER // tn
    return pl.pallas_call(
        _mm_kernel,
        grid=(n // tm, ns, per),
        in_specs=[pl.BlockSpec((None, tm, k), lambda i, s, j: (s, i, 0)),
                  pl.BlockSpec((k, tn), lambda i, s, j: (0, s * per + j))],
        out_specs=pl.BlockSpec((tm, tn), lambda i, s, j: (i, s * per + j)),
        out_shape=jax.ShapeDtypeStruct((n, ns * D_INNER), F32),
        compiler_params=_params(("parallel", "parallel", "parallel")),
        name="matmul_streams",
    )(xs, w)


def _unit_tri_inverses(mats, row, col):
    c = mats[0].shape[0]
    eye = (row == col).astype(F32)
    same = (row // INV_BASE) == (col // INV_BASE)
    ps = [jnp.where(same, a, 0.0) for a in mats]
    xs = [eye + p for p in ps]
    span = 2
    while span < INV_BASE:
        ps = [_bdot(p, p) for p in ps]
        xs = [x + _bdot(x, p) for x, p in zip(xs, ps)]
        span *= 2
    blk = INV_BASE
    while blk < c:
        pair = ((row // (2 * blk)) == (col // (2 * blk))) & jnp.logical_not(
            (row // blk) == (col // blk))
        ts = [_bdot(jnp.where(pair, a, 0.0), x) for a, x in zip(mats, xs)]
        xs = [x + _bdot(x, t) for x, t in zip(xs, ts)]
        blk *= 2
    return xs


def _wkv_kernel(rf_ref, kf_ref, vf_ref, sf_ref, rb_ref, kb_ref, vb_ref, sb_ref,
                w0_ref, w2_ref, a0_ref, a2_ref, kk_ref, ka_ref,
                yf_ref, yb_ref, state_ref, *, chunk, heads):
    c = chunk
    n = RWKV_HEAD
    width = heads * n

    @pl.when(pl.program_id(2) == 0)
    def _():
        state_ref[...] = jnp.zeros_like(state_ref)

    row = lax.broadcasted_iota(jnp.int32, (c, c), 0)
    col = lax.broadcasted_iota(jnp.int32, (c, c), 1)
    lrow = lax.broadcasted_iota(jnp.int32, (width, width), 0)
    lcol = lax.broadcasted_iota(jnp.int32, (width, width), 1)
    seg_ones = ((lrow // n) == (lcol // n)).astype(BF16)

    units = []
    dirs = ((rf_ref, kf_ref, vf_ref, sf_ref), (rb_ref, kb_ref, vb_ref, sb_ref))
    for d, (r_ref, k_ref, v_ref, s_ref) in enumerate(dirs):
        strict = (row > col) if d == 0 else (row < col)
        incl = (row >= col) if d == 0 else (row <= col)
        r = r_ref[...]
        k = k_ref[...]
        v = v_ref[...]
        sm = s_ref[...]
        wl = sm[:, d * LORA:(d + 1) * LORA]
        al = sm[:, (2 + d) * LORA:(3 + d) * LORA]
        w_pre = w0_ref[d:d + 1, :] + _bdot(jnp.tanh(wl), w2_ref[d])
        logw = -EXP_NEG_HALF * _sigmoid(w_pre)
        a = _sigmoid(a0_ref[d:d + 1, :] + _bdot(al, a2_ref[d]))
        kk = k * kk_ref[...]
        ss = _dot_x2(kk * kk, seg_ones)
        kk = kk / jnp.maximum(jnp.sqrt(ss), 1e-12)
        kdir = k * (1.0 + (a - 1.0) * ka_ref[...])
        b = kk * a
        tri = incl.astype(BF16)
        l_hi = logw.astype(BF16)
        l_r = logw - l_hi.astype(F32)
        l_mid = l_r.astype(BF16)
        l_lo = (l_r - l_mid.astype(F32)).astype(BF16)
        cl = (jnp.dot(tri, l_hi, preferred_element_type=F32)
              + jnp.dot(tri, l_mid, preferred_element_type=F32)
              + jnp.dot(tri, l_lo, preferred_element_type=F32))
        cl_end = cl[c - 1:c, :] if d == 0 else cl[0:1, :]
        e_in = jnp.exp(cl)
        e_ex = jnp.exp(cl - logw)
        e_neg = jnp.exp(-cl)
        e_end = jnp.exp(cl_end - cl)
        ones_c = jnp.ones((c, LANES), BF16)
        tn = (((0,), (0,)), ((), ()))
        w_end_t = jnp.exp(lax.dot_general(l_hi, ones_c, tn, preferred_element_type=F32)
                          + lax.dot_general(l_mid, ones_c, tn, preferred_element_type=F32)
                          + lax.dot_general(l_lo, ones_c, tn, preferred_element_type=F32))
        rt = (r * e_in).astype(BF16)
        at = (-kk * e_ex).astype(BF16)
        bt = (b * e_neg).astype(BF16)
        kt = (kdir * e_neg).astype(BF16)
        bh = (b * e_end).astype(BF16)
        kh = (kdir * e_end).astype(BF16)
        vb = v.astype(BF16)
        for h in range(heads):
            sl = slice(h * n, (h + 1) * n)
            units.append(dict(
                d=d, h=h, strict=strict, incl=incl, rt=rt[:, sl], at=at[:, sl], v=vb[:, sl],
                lhs=jnp.concatenate([at[:, sl], rt[:, sl]], axis=0),
                rhs=jnp.concatenate([bt[:, sl], kt[:, sl]], axis=0),
                end=jnp.concatenate([bh[:, sl], kh[:, sl]], axis=0),
                w_end=w_end_t[sl, :n]))

    gs = [_bdot_nt(u["lhs"], u["rhs"]) for u in units]
    a_ab = [jnp.where(u["strict"], g[:c, :c], 0.0) for u, g in zip(units, gs)]
    a_ak = [jnp.where(u["strict"], g[:c, c:], 0.0).astype(BF16) for u, g in zip(units, gs)]
    a_r = [jnp.concatenate([jnp.where(u["incl"], g[c:, :c], 0.0),
                            jnp.where(u["incl"], g[c:, c:], 0.0)], axis=1).astype(BF16)
           for u, g in zip(units, gs)]
    pv = [_bdot(a, u["v"]) for a, u in zip(a_ak, units)]
    tinv = _unit_tri_inverses(a_ab, row, col)
    qs = [_bdot(t, jnp.concatenate([u["at"], p.astype(BF16)], axis=1))
          for t, u, p in zip(tinv, units, pv)]
    zs = [jnp.concatenate([q.astype(BF16),
                           jnp.concatenate([jnp.zeros((c, n), BF16), u["v"]], axis=1)], axis=0)
          for q, u in zip(qs, units)]
    rys = [_bdot(a, z) for a, z in zip(a_r, zs)]
    mss = [_bdot_tn(u["end"], z) for u, z in zip(units, zs)]
    s_old = [state_ref[u["d"], u["h"]] for u in units]
    ys = [_bdot(u["rt"].astype(F32) + ry[:, :n], s) + ry[:, n:]
          for u, ry, s in zip(units, rys, s_old)]
    for u, ms, s in zip(units, mss, s_old):
        state_ref[u["d"], u["h"]] = _bdot(ms[:, :n], s) + u["w_end"] * s + ms[:, n:]
    yf_ref[...] = jnp.concatenate(ys[:heads], axis=1)
    yb_ref[...] = jnp.concatenate(ys[heads:], axis=1)


def _wkv(proj, small, w0, w2, a0, a2, k_k, k_a, batch, seq):
    n_tok = proj.shape[0]
    c = WKV_CHUNK
    hg = WKV_HEADS_PER_STEP
    width = hg * RWKV_HEAD
    nc = seq // c
    groups = D_INNER // width

    def fwd(s):
        return lambda b, g, t: (b * nc + t, s * groups + g)

    def bwd(s):
        return lambda b, g, t: (b * nc + (nc - 1 - t), s * groups + g)

    blk = lambda f: pl.BlockSpec((c, width), f)
    par = lambda shape: pl.BlockSpec(shape, lambda b, g, t: (0,) * (len(shape) - 1) + (g,))
    return pl.pallas_call(
        functools.partial(_wkv_kernel, chunk=c, heads=hg),
        grid=(batch, groups, nc),
        in_specs=[
            blk(fwd(0)), blk(fwd(1)), blk(fwd(2)),
            pl.BlockSpec((c, 4 * LORA), lambda b, g, t: (b * nc + t, 0)),
            blk(bwd(0)), blk(bwd(1)), blk(bwd(2)),
            pl.BlockSpec((c, 4 * LORA), lambda b, g, t: (b * nc + (nc - 1 - t), 0)),
            par((2, width)), par((2, LORA, width)), par((2, width)), par((2, LORA, width)),
            par((1, width)), par((1, width)),
        ],
        out_specs=[
            pl.BlockSpec((c, width), lambda b, g, t: (b * nc + t, g)),
            pl.BlockSpec((c, width), lambda b, g, t: (b * nc + (nc - 1 - t), g)),
        ],
        out_shape=[jax.ShapeDtypeStruct((n_tok, D_INNER), F32),
                   jax.ShapeDtypeStruct((n_tok, D_INNER), F32)],
        scratch_shapes=[pltpu.VMEM((2, hg, RWKV_HEAD, RWKV_HEAD), F32)],
        compiler_params=_params(("parallel", "parallel", "arbitrary")),
        name="wkv7_chunked",
    )(proj, proj, proj, small, proj, proj, proj, small, w0, w2, a0, a2, k_k, k_a)


def _rw_post_kernel(yf_ref, yb_ref, r_ref, k_ref, v_ref, g_ref, s_ref, x_ref,
                    a0_ref, a2_ref, ka_ref, rk_ref, lg_ref, lb_ref, ind_ref, indt_ref, wo_ref,
                    o_ref):
    ind = ind_ref[...]
    indt = indt_ref[...]

    def seg_sum(z):
        return _dot_x2(_dot_x2(z, ind), indt)

    y = yf_ref[...] + yb_ref[...]
    mean = seg_sum(y) * (1.0 / RWKV_HEAD)
    yc = y - mean
    var = seg_sum(yc * yc) * (1.0 / RWKV_HEAD)
    yn = yc * lax.rsqrt(var + LN_X_EPS) * lg_ref[...] + lb_ref[...]
    sm = s_ref[...]
    a_sum = (_sigmoid(a0_ref[0:1, :] + _bdot(sm[:, 2 * LORA:3 * LORA], a2_ref[0]))
             + _sigmoid(a0_ref[1:2, :] + _bdot(sm[:, 3 * LORA:4 * LORA], a2_ref[1])))
    k_sum = k_ref[...] * (2.0 + (a_sum - 2.0) * ka_ref[...])
    bonus = seg_sum(r_ref[...] * k_sum * rk_ref[...]) * v_ref[...]
    g = g_ref[...]
    out = (yn + bonus) * (g * _sigmoid(g))
    o_ref[...] = x_ref[...] + _bdot(out, wo_ref[...])


def _rw_post(yf, yb, proj, small, x, a0, a2, k_a, r_k, lnx_g, lnx_b, ind, indt, w_out, tm=128):
    n = x.shape[0]
    e = D_INNER
    row = lambda w, j: pl.BlockSpec((tm, w), lambda i: (i, j))
    full = lambda shape: pl.BlockSpec(shape, lambda i: (0,) * len(shape))
    return pl.pallas_call(
        _rw_post_kernel,
        grid=(n // tm,),
        in_specs=[
            row(e, 0), row(e, 0),
            row(e, 0), row(e, 1), row(e, 2), row(e, 3),
            row(4 * LORA, 0), row(D_MODEL, 0),
            full((2, e)), full((2, LORA, e)), full((1, e)), full((1, e)), full((1, e)),
            full((1, e)), full((e, LANES)), full((LANES, e)), full((e, D_MODEL)),
        ],
        out_specs=row(D_MODEL, 0),
        out_shape=jax.ShapeDtypeStruct((n, D_MODEL), F32),
        compiler_params=_params(("parallel",)),
        name="rw_post",
    )(yf, yb, proj, proj, proj, proj, small, x, a0, a2, k_a, r_k, lnx_g, lnx_b, ind, indt, w_out)


def _ml_up_kernel(s_ref, cs_ref, sn_ref, qn_ref, kvn_ref, wq_ref, wkv_ref,
                  q_ref, k_ref, v_ref):
    sm = s_ref[...]
    cs = cs_ref[...]
    sn = sn_ref[...]
    scale = QK_HEAD ** -0.5 * LOG2_E

    def rms(x, g):
        return x * lax.rsqrt(jnp.mean(x * x, axis=-1, keepdims=True) + NORM_EPS) * g

    def rope(x2):
        return x2 * cs + pltpu.roll(x2, QK_ROPE, axis=1) * sn

    cq = rms(sm[:, :Q_LORA], qn_ref[...])
    ckv = rms(sm[:, Q_LORA:Q_LORA + KV_LORA], kvn_ref[...])
    q = _bdot(cq, wq_ref[...])
    kv = _bdot(ckv, wkv_ref[...])
    kpe = rope(sm[:, Q_LORA + KV_LORA:]).astype(BF16)
    lane = lax.broadcasted_iota(jnp.int32, kpe.shape, 1)
    ones_col = (lane == 0).astype(BF16)
    hv = MLA_HEADS * QK_NOPE
    for h in range(MLA_HEADS):
        o = h * Q_PAD
        q_ref[:, o:o + QK_NOPE] = (q[:, o:o + QK_NOPE] * scale).astype(BF16)
        q_ref[:, o + QK_NOPE:o + Q_PAD] = (rope(q[:, o + QK_NOPE:o + Q_PAD]) * scale).astype(BF16)
        k_ref[:, o:o + QK_NOPE] = kv[:, h * QK_NOPE:(h + 1) * QK_NOPE].astype(BF16)
        k_ref[:, o + QK_NOPE:o + Q_PAD] = kpe
        v_ref[:, o:o + V_HEAD] = kv[:, hv + h * V_HEAD:hv + (h + 1) * V_HEAD].astype(BF16)
        v_ref[:, o + V_HEAD:o + Q_PAD] = ones_col


def _ml_up(small, cs, sn, qn, kvn, wq, wkv, seq, tm=256):
    n = small.shape[0]
    per_seq = seq // tm
    row = lambda w: pl.BlockSpec((tm, w), lambda i: (i, 0))
    pos = pl.BlockSpec((tm, LANES), lambda i: (i % per_seq, 0))
    full = lambda shape: pl.BlockSpec(shape, lambda i: (0,) * len(shape))
    hq = MLA_HEADS * Q_PAD
    return pl.pallas_call(
        _ml_up_kernel,
        grid=(n // tm,),
        in_specs=[row(SMALL_COLS), pos, pos, full((1, Q_LORA)), full((1, KV_LORA)),
                  full((Q_LORA, hq)), full((KV_LORA, 2 * MLA_HEADS * V_HEAD))],
        out_specs=[row(hq), row(hq), row(hq)],
        out_shape=[jax.ShapeDtypeStruct((n, hq), BF16)] * 3,
        compiler_params=_params(("parallel",)),
        name="mla_up",
    )(small, cs, sn, qn, kvn, wq, wkv)


def _attn_kernel(q_ref, k_ref, v_ref, o_ref, m_ref, jump_ref, acc_ref, *, tk, parts):
    tq = q_ref.shape[0]
    rows = tq // parts
    nk = k_ref.shape[0] // tk
    nt = (((1,), (1,)), ((), ()))
    sl = [slice(a * rows, (a + 1) * rows) for a in range(parts)]
    lane_tiles = tk // LANES

    def wide(x, reps):
        return jnp.concatenate([x] * reps, axis=1)

    def row_max(s):
        return jnp.broadcast_to(jnp.max(s, axis=1, keepdims=True), (s.shape[0], LANES))

    def tiles(j):
        off = pl.multiple_of(j * tk, tk)
        return k_ref[pl.ds(off, tk), :], v_ref[pl.ds(off, tk), :]

    def scores(kj):
        return [lax.dot_general(q_ref[r, :], kj, nt, preferred_element_type=F32) for r in sl]

    def finish():
        acc = acc_ref[...]
        o_ref[...] = (acc[:, :V_HEAD] / acc[:, V_HEAD:V_HEAD + 1]).astype(o_ref.dtype)

    m_ref[...] = row_max(lax.dot_general(q_ref[...], k_ref[0:LANES, :], nt,
                                         preferred_element_type=F32))
    jump_ref[...] = jnp.zeros_like(jump_ref)
    acc_ref[...] = jnp.zeros_like(acc_ref)

    def fast(j, carry):
        kj, vj = tiles(j)
        ss = scores(kj)
        m_old = [m_ref[r, :] for r in sl]
        ps = [jnp.exp2(s - wide(m, lane_tiles)).astype(BF16) for s, m in zip(ss, m_old)]
        pv = [jnp.dot(p, vj, preferred_element_type=F32) for p in ps]
        m_tile = [row_max(s) for s in ss]
        for r, mo, mt, x in zip(sl, m_old, m_tile, pv):
            mn = jnp.maximum(mo, mt)
            acc_ref[r, :] = (acc_ref[r, :] + x) * wide(jnp.exp2(mo - mn), Q_PAD // LANES)
            jump_ref[r, :] = jnp.maximum(jump_ref[r, :], mt - mo)
            m_ref[r, :] = mn
        return carry

    lax.fori_loop(0, nk, fast, 0)
    safe = jnp.max(jump_ref[...]) <= MAX_JUMP

    @pl.when(safe)
    def _():
        finish()

    @pl.when(jnp.logical_not(safe))
    def _():
        m_ref[...] = jnp.full_like(m_ref, -jnp.inf)
        acc_ref[...] = jnp.zeros_like(acc_ref)

        def exact(j, carry):
            kj, vj = tiles(j)
            ss = scores(kj)
            for r, s in zip(sl, ss):
                mo = m_ref[r, :]
                mn = jnp.maximum(mo, row_max(s))
                p = jnp.exp2(s - wide(mn, lane_tiles)).astype(BF16)
                acc_ref[r, :] = (acc_ref[r, :] * wide(jnp.exp2(mo - mn), Q_PAD // LANES)
                                 + jnp.dot(p, vj, preferred_element_type=F32))
                m_ref[r, :] = mn
            return carry

        lax.fori_loop(0, nk, exact, 0)
        finish()


def _attention(q, k, v, batch, seq, tq=1024, tk=512, parts=2):
    n = q.shape[0]
    tq = min(tq, seq)
    nq = seq // tq
    return pl.pallas_call(
        functools.partial(_attn_kernel, tk=tk, parts=parts),
        grid=(batch, MLA_HEADS, nq),
        in_specs=[
            pl.BlockSpec((tq, Q_PAD), lambda b, h, i: (b * nq + i, h)),
            pl.BlockSpec((seq, Q_PAD), lambda b, h, i: (b, h)),
            pl.BlockSpec((seq, Q_PAD), lambda b, h, i: (b, h)),
        ],
        out_specs=pl.BlockSpec((tq, V_HEAD), lambda b, h, i: (b * nq + i, h)),
        out_shape=jax.ShapeDtypeStruct((n, MLA_HEADS * V_HEAD), BF16),
        scratch_shapes=[pltpu.VMEM((tq, LANES), F32), pltpu.VMEM((tq, LANES), F32),
                        pltpu.VMEM((tq, Q_PAD), F32)],
        compiler_params=_params(("parallel", "parallel", "parallel")),
        name="mla_flash_attention",
    )(q, k, v)


def _ml_post_kernel(o_ref, g_ref, x_ref, wo_ref, fg_ref, y_ref, *, final):
    g = g_ref[...]
    out = o_ref[...].astype(F32) * (g * _sigmoid(g))
    y = x_ref[...] + _bdot(out, wo_ref[...])
    if final:
        y = _rms(y, fg_ref[...])
    y_ref[...] = y


def _ml_post(o, g, x, w_out, final_g, final, tm=256):
    n = x.shape[0]
    row = lambda w: pl.BlockSpec((tm, w), lambda i: (i, 0))
    full = lambda shape: pl.BlockSpec(shape, lambda i: (0,) * len(shape))
    return pl.pallas_call(
        functools.partial(_ml_post_kernel, final=final),
        grid=(n // tm,),
        in_specs=[row(D_INNER), row(D_INNER), row(D_MODEL), full((D_INNER, D_MODEL)),
                  full((1, D_MODEL))],
        out_specs=row(D_MODEL),
        out_shape=jax.ShapeDtypeStruct((n, D_MODEL), F32),
        compiler_params=_params(("parallel",)),
        name="mla_post",
    )(o, g, x, w_out, final_g)


def _rotate_half_cols(w):
    half = QK_ROPE // 2
    return jnp.concatenate([-w[..., half:], w[..., :half]], axis=-1)


def _prep_rwkv(j, rw_in, rw_w0, rw_w2, rw_a0, rw_a2, rw_kk, rw_ka, rw_rk, rw_lnx_g, rw_lnx_b,
               rw_out):
    e = D_INNER
    w = rw_in[j]
    w_main = w[:, :4 * e].astype(BF16)
    ww = w[:, 4 * e:4 * e + 2 * LORA]
    wa = w[:, 4 * e + 2 * LORA:]
    z = jnp.zeros_like(ww)
    w_small = jnp.concatenate([jnp.concatenate([ww, z], axis=1),
                               jnp.concatenate([z, wa], axis=1)], axis=0).astype(BF16)
    return dict(
        w_main=w_main, w_small=w_small, w0=rw_w0[j], w2=rw_w2[j].astype(BF16), a0=rw_a0[j],
        a2=rw_a2[j].astype(BF16), k_k=rw_kk[j][None], k_a=rw_ka[j][None],
        r_k=rw_rk[j].reshape(1, e), lnx_g=rw_lnx_g[j][None], lnx_b=rw_lnx_b[j][None],
        w_out=rw_out[j].astype(BF16))


def _prep_mla(j, ml_in, ml_qn, ml_kvn, ml_uq, ml_ukv, ml_out):
    w = ml_in[j]
    lo = Q_LORA + KV_LORA
    w_kpe = w[:, lo:lo + QK_ROPE]
    w_small = jnp.concatenate([w[:, :lo], w_kpe, _rotate_half_cols(w_kpe)], axis=1).astype(BF16)
    w_g = w[:, lo + QK_ROPE:].astype(BF16)
    uq = ml_uq[j].reshape(Q_LORA, MLA_HEADS, QK_HEAD)
    pe = uq[..., QK_NOPE:]
    wq = jnp.concatenate([uq[..., :QK_NOPE], pe, _rotate_half_cols(pe)], axis=-1)
    wq = wq.reshape(Q_LORA, MLA_HEADS * Q_PAD).astype(BF16)
    ukv = ml_ukv[j].reshape(KV_LORA, MLA_HEADS, QK_NOPE + V_HEAD)
    wkv = jnp.concatenate([ukv[..., :QK_NOPE].reshape(KV_LORA, -1),
                           ukv[..., QK_NOPE:].reshape(KV_LORA, -1)], axis=1).astype(BF16)
    return dict(w_small=w_small, w_g=w_g, qn=ml_qn[j][None], kvn=ml_kvn[j][None], wq=wq,
                wkv=wkv, w_out=ml_out[j].astype(BF16))


def _rope_tables(seq):
    inv_freq = 1.0 / (ROPE_THETA ** (jnp.arange(0, QK_ROPE, 2, dtype=F32) / QK_ROPE))
    ang = jnp.arange(seq, dtype=F32)[:, None] * inv_freq[None, :]
    z = jnp.zeros((seq, LANES - QK_ROPE), F32)
    cs = jnp.concatenate([jnp.cos(ang), jnp.cos(ang), z], axis=1)
    sn = jnp.concatenate([jnp.sin(ang), jnp.sin(ang), z], axis=1)
    return cs, sn


def _head_indicator():
    e = lax.broadcasted_iota(jnp.int32, (D_INNER, LANES), 0)
    h = lax.broadcasted_iota(jnp.int32, (D_INNER, LANES), 1)
    ind = ((e // RWKV_HEAD) == h).astype(BF16)
    return ind, ind.T


def _trunk(x3, ln_g, final_g, rw, ml):
    batch, seq, _ = x3.shape
    x = x3.reshape(batch * seq, D_MODEL)
    ind, indt = _head_indicator()
    cs, sn = _rope_tables(seq)
    for i in range(DEPTH):
        g = ln_g[i][None]
        if i % 2 == 0:
            p = rw[i // 2]
            xs4, xs2 = _rw_prep(x, g, p["mu"], seq)
            proj = _matmul_streams(xs4, p["w_main"])
            small = _matmul(xs2, p["w_small"])
            yf, yb = _wkv(proj, small, p["w0"], p["w2"], p["a0"], p["a2"], p["k_k"], p["k_a"],
                          batch, seq)
            x = _rw_post(yf, yb, proj, small, x, p["a0"], p["a2"], p["k_a"], p["r_k"],
                         p["lnx_g"], p["lnx_b"], ind, indt, p["w_out"])
        else:
            p = ml[i // 2]
            h = _rmsnorm_bf16(x, g)
            small = _matmul(h, p["w_small"], tn=SMALL_COLS)
            gate = _matmul(h, p["w_g"])
            q, k, v = _ml_up(small, cs, sn, p["qn"], p["kvn"], p["wq"], p["wkv"], seq)
            o = _attention(q, k, v, batch, seq)
            x = _ml_post(o, gate, x, p["w_out"], final_g[None], final=(i == DEPTH - 1))
    return x.reshape(batch, seq, D_MODEL)


def kernel(x_prompt, x_sample, ln_g, final_g, rw_mu, rw_in, rw_w0, rw_w2, rw_a0, rw_a2, rw_kk,
           rw_ka, rw_rk, rw_lnx_g, rw_lnx_b, rw_out, ml_in, ml_qn, ml_kvn, ml_uq, ml_ukv, ml_out):
    rw = []
    for j in range(rw_in.shape[0]):
        p = _prep_rwkv(j, rw_in, rw_w0, rw_w2, rw_a0, rw_a2, rw_kk, rw_ka, rw_rk, rw_lnx_g,
                       rw_lnx_b, rw_out)
        p["mu"] = rw_mu[j]
        rw.append(p)
    ml = [_prep_mla(j, ml_in, ml_qn, ml_kvn, ml_uq, ml_ukv, ml_out) for j in range(ml_in.shape[0])]
    y_prompt = _trunk(x_prompt, ln_g, final_g, rw, ml)
    y_sample = _trunk(x_sample, ln_g, final_g, rw, ml)
    return (y_prompt, y_sample)
```

```python
import functools

import jax
import jax.numpy as jnp
from jax import lax
from jax.experimental import pallas as pl
from jax.experimental.pallas import tpu as pltpu

F32 = jnp.float32
BF16 = jnp.bfloat16

D_MODEL = 1024
D_INNER = 2048
DEPTH = 4
RWKV_HEAD = 64
RWKV_HEADS = D_INNER // RWKV_HEAD
LORA = 64
LN_X_EPS = 64e-5
NORM_EPS = 1e-6
MLA_HEADS = 16
QK_NOPE = 128
QK_ROPE = 64
V_HEAD = 128
QK_HEAD = QK_NOPE + QK_ROPE
Q_LORA = 384
KV_LORA = 256
ROPE_THETA = 10000.0
Q_PAD = 256
SMALL_COLS = Q_LORA + KV_LORA + 2 * QK_ROPE

LANES = 128
SUBLANES = 8
WKV_CHUNK = 64
WKV_HEADS_PER_STEP = 8
INV_BASE = 16
EXP_NEG_HALF = 0.6065306597126334
LOG2_E = 1.4426950408889634
VMEM_LIMIT = 48 * 1024 * 1024

HIGHEST = lax.Precision.HIGHEST


def _bdot(a, b):
    return jnp.dot(a.astype(BF16), b.astype(BF16), preferred_element_type=F32)


def _bdot_nt(a, b):
    return lax.dot_general(a.astype(BF16), b.astype(BF16), (((1,), (1,)), ((), ())),
                           preferred_element_type=F32)


def _bdot_tn(a, b):
    return lax.dot_general(a.astype(BF16), b.astype(BF16), (((0,), (0,)), ((), ())),
                           preferred_element_type=F32)


def _split2(x):
    hi = x.astype(BF16)
    lo = (x - hi.astype(F32)).astype(BF16)
    return hi, lo


def _dot_x2(x, m_bf16):
    hi, lo = _split2(x)
    return (jnp.dot(hi, m_bf16, preferred_element_type=F32)
            + jnp.dot(lo, m_bf16, preferred_element_type=F32))


def _sigmoid(x):
    return 1.0 / (1.0 + jnp.exp(-x))


def _params(sem):
    return pltpu.CompilerParams(dimension_semantics=sem, vmem_limit_bytes=VMEM_LIMIT)


def _rms(x, g):
    return x * lax.rsqrt(jnp.mean(x * x, axis=-1, keepdims=True) + NORM_EPS) * g


def _rw_prep_kernel(x_ref, xp_ref, xn_ref, g_ref, mu_ref, xs4_ref, xs2_ref, *, tm, seq):
    i = pl.program_id(0)
    g = g_ref[...]
    h = _rms(x_ref[...], g)
    hp = _rms(xp_ref[SUBLANES - 1:SUBLANES, :], g)
    hn = _rms(xn_ref[0:1, :], g)
    row0 = i * tm
    hp = jnp.where((row0 % seq) == 0, 0.0, hp)
    hn = jnp.where(((row0 + tm) % seq) == 0, 0.0, hn)
    rid = lax.broadcasted_iota(jnp.int32, (tm, 1), 0)
    prev = jnp.where(rid == 0, hp, pltpu.roll(h, 1, axis=0))
    nxt = jnp.where(rid == tm - 1, hn, pltpu.roll(h, tm - 1, axis=0))
    xx = 0.5 * (prev + nxt) - h
    mu = mu_ref[...]
    for n, s in enumerate((0, 2, 3, 5)):
        xs4_ref[n] = (h + mu[s:s + 1, :] * xx).astype(BF16)
    xs2_ref[:, :D_MODEL] = (h + mu[1:2, :] * xx).astype(BF16)
    xs2_ref[:, D_MODEL:] = (h + mu[4:5, :] * xx).astype(BF16)


def _rw_prep(x, ln_g, mu, seq, tm=256):
    n = x.shape[0]
    nb8 = n // SUBLANES
    per = tm // SUBLANES
    return pl.pallas_call(
        functools.partial(_rw_prep_kernel, tm=tm, seq=seq),
        grid=(n // tm,),
        in_specs=[
            pl.BlockSpec((tm, D_MODEL), lambda i: (i, 0)),
            pl.BlockSpec((SUBLANES, D_MODEL), lambda i: (jnp.maximum(i * per - 1, 0), 0)),
            pl.BlockSpec((SUBLANES, D_MODEL), lambda i: (jnp.minimum((i + 1) * per, nb8 - 1), 0)),
            pl.BlockSpec((1, D_MODEL), lambda i: (0, 0)),
            pl.BlockSpec((6, D_MODEL), lambda i: (0, 0)),
        ],
        out_specs=[
            pl.BlockSpec((4, tm, D_MODEL), lambda i: (0, i, 0)),
            pl.BlockSpec((tm, 2 * D_MODEL), lambda i: (i, 0)),
        ],
        out_shape=[
            jax.ShapeDtypeStruct((4, n, D_MODEL), BF16),
            jax.ShapeDtypeStruct((n, 2 * D_MODEL), BF16),
        ],
        compiler_params=_params(("parallel",)),
        name="rw_prep",
    )(x, x, x, ln_g, mu)


def _norm_kernel(x_ref, g_ref, o_ref):
    o_ref[...] = _rms(x_ref[...], g_ref[...]).astype(o_ref.dtype)


def _rmsnorm_bf16(x, g, tm=512):
    n = x.shape[0]
    return pl.pallas_call(
        _norm_kernel,
        grid=(n // tm,),
        in_specs=[pl.BlockSpec((tm, D_MODEL), lambda i: (i, 0)),
                  pl.BlockSpec((1, D_MODEL), lambda i: (0, 0))],
        out_specs=pl.BlockSpec((tm, D_MODEL), lambda i: (i, 0)),
        out_shape=jax.ShapeDtypeStruct((n, D_MODEL), BF16),
        compiler_params=_params(("parallel",)),
        name="rmsnorm",
    )(x, g)


def _mm_kernel(x_ref, w_ref, o_ref):
    o_ref[...] = jnp.dot(x_ref[...], w_ref[...], preferred_element_type=F32).astype(o_ref.dtype)


def _matmul(x, w, tm=512, tn=512):
    n, k = x.shape
    m = w.shape[1]
    tn = min(tn, m)
    return pl.pallas_call(
        _mm_kernel,
        grid=(n // tm, m // tn),
        in_specs=[pl.BlockSpec((tm, k), lambda i, j: (i, 0)),
                  pl.BlockSpec((k, tn), lambda i, j: (0, j))],
        out_specs=pl.BlockSpec((tm, tn), lambda i, j: (i, j)),
        out_shape=jax.ShapeDtypeStruct((n, m), F32),
        compiler_params=_params(("parallel", "parallel")),
        name="matmul",
    )(x, w)


def _matmul_streams(xs, w, tm=1024, tn=512):
    ns, n, k = xs.shape
    per = D_INNER // tn
    return pl.pallas_call(
        _mm_kernel,
        grid=(n // tm, ns, per),
        in_specs=[pl.BlockSpec((None, tm, k), lambda i, s, j: (s, i, 0)),
                  pl.BlockSpec((k, tn), lambda i, s, j: (0, s * per + j))],
        out_specs=pl.BlockSpec((tm, tn), lambda i, s, j: (i, s * per + j)),
        out_shape=jax.ShapeDtypeStruct((n, ns * D_INNER), BF16),
        compiler_params=_params(("parallel", "parallel", "parallel")),
        name="matmul_streams",
    )(xs, w)


def _unit_tri_inverses(mats, row, col):
    c = mats[0].shape[0]
    eye = (row == col).astype(F32)
    same = (row // INV_BASE) == (col // INV_BASE)
    ps = [jnp.where(same, a, 0.0) for a in mats]
    xs = [eye + p for p in ps]
    span = 2
    while span < INV_BASE:
        ps = [_bdot(p, p) for p in ps]
        xs = [x + _bdot(x, p) for x, p in zip(xs, ps)]
        span *= 2
    blk = INV_BASE
    while blk < c:
        pair = ((row // (2 * blk)) == (col // (2 * blk))) & jnp.logical_not(
            (row // blk) == (col // blk))
        ts = [_bdot(jnp.where(pair, a, 0.0), x) for a, x in zip(mats, xs)]
        xs = [x + _bdot(x, t) for x, t in zip(xs, ts)]
        blk *= 2
    return xs


def _wkv_kernel(rf_ref, kf_ref, vf_ref, sf_ref, rb_ref, kb_ref, vb_ref, sb_ref,
                w0_ref, w2_ref, a0_ref, a2_ref, kk_ref, ka_ref,
                yf_ref, yb_ref, state_ref, *, chunk, heads):
    c = chunk
    n = RWKV_HEAD
    width = heads * n

    @pl.when(pl.program_id(2) == 0)
    def _():
        state_ref[...] = jnp.zeros_like(state_ref)

    row = lax.broadcasted_iota(jnp.int32, (c, c), 0)
    col = lax.broadcasted_iota(jnp.int32, (c, c), 1)
    lrow = lax.broadcasted_iota(jnp.int32, (width, width), 0)
    lcol = lax.broadcasted_iota(jnp.int32, (width, width), 1)
    seg_ones = ((lrow // n) == (lcol // n)).astype(BF16)

    units = []
    dirs = ((rf_ref, kf_ref, vf_ref, sf_ref), (rb_ref, kb_ref, vb_ref, sb_ref))
    for d, (r_ref, k_ref, v_ref, s_ref) in enumerate(dirs):
        strict = (row > col) if d == 0 else (row < col)
        incl = (row >= col) if d == 0 else (row <= col)
        r = r_ref[...].astype(F32)
        k = k_ref[...].astype(F32)
        v = v_ref[...]
        sm = s_ref[...]
        wl = sm[:, d * LORA:(d + 1) * LORA]
        al = sm[:, (2 + d) * LORA:(3 + d) * LORA]
        w_pre = w0_ref[d:d + 1, :] + _bdot(jnp.tanh(wl), w2_ref[d])
        logw = -EXP_NEG_HALF * _sigmoid(w_pre)
        a = _sigmoid(a0_ref[d:d + 1, :] + _bdot(al, a2_ref[d]))
        kk = k * kk_ref[...]
        ss = _dot_x2(kk * kk, seg_ones)
        kk = kk / jnp.maximum(jnp.sqrt(ss), 1e-12)
        kdir = k * (1.0 + (a - 1.0) * ka_ref[...])
        b = kk * a
        tri = incl.astype(BF16)
        l_hi = logw.astype(BF16)
        l_r = logw - l_hi.astype(F32)
        l_mid = l_r.astype(BF16)
        l_lo = (l_r - l_mid.astype(F32)).astype(BF16)
        cl = (jnp.dot(tri, l_hi, preferred_element_type=F32)
              + jnp.dot(tri, l_mid, preferred_element_type=F32)
              + jnp.dot(tri, l_lo, preferred_element_type=F32))
        cl_end = cl[c - 1:c, :] if d == 0 else cl[0:1, :]
        e_in = jnp.exp(cl)
        e_ex = jnp.exp(cl - logw)
        e_neg = jnp.exp(-cl)
        e_end = jnp.exp(cl_end - cl)
        ones_c = jnp.ones((c, LANES), BF16)
        tn = (((0,), (0,)), ((), ()))
        w_end_t = jnp.exp(lax.dot_general(l_hi, ones_c, tn, preferred_element_type=F32)
                          + lax.dot_general(l_mid, ones_c, tn, preferred_element_type=F32)
                          + lax.dot_general(l_lo, ones_c, tn, preferred_element_type=F32))
        rt = (r * e_in).astype(BF16)
        at = (-kk * e_ex).astype(BF16)
        bt = (b * e_neg).astype(BF16)
        kt = (kdir * e_neg).astype(BF16)
        bh = (b * e_end).astype(BF16)
        kh = (kdir * e_end).astype(BF16)
        vb = v.astype(BF16)
        for h in range(heads):
            sl = slice(h * n, (h + 1) * n)
            units.append(dict(
                d=d, h=h, strict=strict, incl=incl, rt=rt[:, sl], at=at[:, sl], v=vb[:, sl],
                lhs=jnp.concatenate([at[:, sl], rt[:, sl]], axis=0),
                rhs=jnp.concatenate([bt[:, sl], kt[:, sl]], axis=0),
                end=jnp.concatenate([bh[:, sl], kh[:, sl]], axis=0),
                w_end=w_end_t[sl, :n]))

    gs = [_bdot_nt(u["lhs"], u["rhs"]) for u in units]
    a_ab = [jnp.where(u["strict"], g[:c, :c], 0.0) for u, g in zip(units, gs)]
    a_ak = [jnp.where(u["strict"], g[:c, c:], 0.0).astype(BF16) for u, g in zip(units, gs)]
    a_r = [jnp.concatenate([jnp.where(u["incl"], g[c:, :c], 0.0),
                            jnp.where(u["incl"], g[c:, c:], 0.0)], axis=1).astype(BF16)
           for u, g in zip(units, gs)]
    pv = [_bdot(a, u["v"]) for a, u in zip(a_ak, units)]
    tinv = _unit_tri_inverses(a_ab, row, col)
    qs = [_bdot(t, jnp.concatenate([u["at"], p.astype(BF16)], axis=1))
          for t, u, p in zip(tinv, units, pv)]
    zs = [jnp.concatenate([q.astype(BF16),
                           jnp.concatenate([jnp.zeros((c, n), BF16), u["v"]], axis=1)], axis=0)
          for q, u in zip(qs, units)]
    rys = [_bdot(a, z) for a, z in zip(a_r, zs)]
    mss = [_bdot_tn(u["end"], z) for u, z in zip(units, zs)]
    s_old = [state_ref[u["d"], u["h"]] for u in units]
    ys = [_bdot(u["rt"].astype(F32) + ry[:, :n], s) + ry[:, n:]
          for u, ry, s in zip(units, rys, s_old)]
    for u, ms, s in zip(units, mss, s_old):
        state_ref[u["d"], u["h"]] = _bdot(ms[:, :n], s) + u["w_end"] * s + ms[:, n:]
    yf_ref[...] = jnp.concatenate(ys[:heads], axis=1).astype(yf_ref.dtype)
    yb_ref[...] = jnp.concatenate(ys[heads:], axis=1).astype(yb_ref.dtype)


def _wkv(proj, small, w0, w2, a0, a2, k_k, k_a, batch, seq):
    n_tok = proj.shape[0]
    c = WKV_CHUNK
    hg = WKV_HEADS_PER_STEP
    width = hg * RWKV_HEAD
    nc = seq // c
    groups = D_INNER // width

    def fwd(s):
        return lambda b, g, t: (b * nc + t, s * groups + g)

    def bwd(s):
        return lambda b, g, t: (b * nc + (nc - 1 - t), s * groups + g)

    blk = lambda f: pl.BlockSpec((c, width), f)
    par = lambda shape: pl.BlockSpec(shape, lambda b, g, t: (0,) * (len(shape) - 1) + (g,))
    return pl.pallas_call(
        functools.partial(_wkv_kernel, chunk=c, heads=hg),
        grid=(batch, groups, nc),
        in_specs=[
            blk(fwd(0)), blk(fwd(1)), blk(fwd(2)),
            pl.BlockSpec((c, 4 * LORA), lambda b, g, t: (b * nc + t, 0)),
            blk(bwd(0)), blk(bwd(1)), blk(bwd(2)),
            pl.BlockSpec((c, 4 * LORA), lambda b, g, t: (b * nc + (nc - 1 - t), 0)),
            par((2, width)), par((2, LORA, width)), par((2, width)), par((2, LORA, width)),
            par((1, width)), par((1, width)),
        ],
        out_specs=[
            pl.BlockSpec((c, width), lambda b, g, t: (b * nc + t, g)),
            pl.BlockSpec((c, width), lambda b, g, t: (b * nc + (nc - 1 - t), g)),
        ],
        out_shape=[jax.ShapeDtypeStruct((n_tok, D_INNER), BF16),
                   jax.ShapeDtypeStruct((n_tok, D_INNER), BF16)],
        scratch_shapes=[pltpu.VMEM((2, hg, RWKV_HEAD, RWKV_HEAD), F32)],
        compiler_params=_params(("parallel", "parallel", "arbitrary")),
        name="wkv7_chunked",
    )(proj, proj, proj, small, proj, proj, proj, small, w0, w2, a0, a2, k_k, k_a)


def _rw_post_kernel(yf_ref, yb_ref, r_ref, k_ref, v_ref, g_ref, s_ref, x_ref,
                    a0_ref, a2_ref, ka_ref, rk_ref, lg_ref, lb_ref, ind_ref, indt_ref, wo_ref,
                    o_ref):
    ind = ind_ref[...]
    indt = indt_ref[...]

    y = yf_ref[...].astype(F32) + yb_ref[...].astype(F32)
    mean = _dot_x2(_dot_x2(y, ind) * (1.0 / RWKV_HEAD), indt)
    yc = y - mean
    var = _bdot(yc * yc, ind) * (1.0 / RWKV_HEAD)
    rstd = _bdot(lax.rsqrt(var + LN_X_EPS), indt)
    yn = yc * rstd * lg_ref[...] + lb_ref[...]
    sm = s_ref[...]
    a_sum = (_sigmoid(a0_ref[0:1, :] + _bdot(sm[:, 2 * LORA:3 * LORA], a2_ref[0]))
             + _sigmoid(a0_ref[1:2, :] + _bdot(sm[:, 3 * LORA:4 * LORA], a2_ref[1])))
    k_sum = k_ref[...].astype(F32) * (2.0 + (a_sum - 2.0) * ka_ref[...])
    coef = _bdot(_bdot(r_ref[...].astype(F32) * k_sum * rk_ref[...], ind), indt)
    g = g_ref[...].astype(F32)
    out = (yn + coef * v_ref[...].astype(F32)) * (g * _sigmoid(g))
    o_ref[...] = x_ref[...] + _bdot(out, wo_ref[...])


def _rw_post(yf, yb, proj, small, x, a0, a2, k_a, r_k, lnx_g, lnx_b, ind, indt, w_out, tm=256):
    n = x.shape[0]
    e = D_INNER
    row = lambda w, j: pl.BlockSpec((tm, w), lambda i: (i, j))
    full = lambda shape: pl.BlockSpec(shape, lambda i: (0,) * len(shape))
    return pl.pallas_call(
        _rw_post_kernel,
        grid=(n // tm,),
        in_specs=[
            row(e, 0), row(e, 0),
            row(e, 0), row(e, 1), row(e, 2), row(e, 3),
            row(4 * LORA, 0), row(D_MODEL, 0),
            full((2, e)), full((2, LORA, e)), full((1, e)), full((1, e)), full((1, e)),
            full((1, e)), full((e, LANES)), full((LANES, e)), full((e, D_MODEL)),
        ],
        out_specs=row(D_MODEL, 0),
        out_shape=jax.ShapeDtypeStruct((n, D_MODEL), F32),
        compiler_params=_params(("parallel",)),
        name="rw_post",
    )(yf, yb, proj, proj, proj, proj, small, x, a0, a2, k_a, r_k, lnx_g, lnx_b, ind, indt, w_out)


def _ml_up_kernel(s_ref, cs_ref, sn_ref, qn_ref, kvn_ref, wq_ref, wkv_ref,
                  q_ref, k_ref, v_ref):
    sm = s_ref[...]
    cs = cs_ref[...]
    sn = sn_ref[...]
    scale = QK_HEAD ** -0.5 * LOG2_E

    def rms(x, g):
        return x * lax.rsqrt(jnp.mean(x * x, axis=-1, keepdims=True) + NORM_EPS) * g

    def rope(x2):
        return x2 * cs + pltpu.roll(x2, QK_ROPE, axis=1) * sn

    cq = rms(sm[:, :Q_LORA], qn_ref[...])
    ckv = rms(sm[:, Q_LORA:Q_LORA + KV_LORA], kvn_ref[...])
    q = _bdot(cq, wq_ref[...])
    kv = _bdot(ckv, wkv_ref[...])
    kpe = rope(sm[:, Q_LORA + KV_LORA:]).astype(BF16)
    lane = lax.broadcasted_iota(jnp.int32, kpe.shape, 1)
    ones_col = (lane == 0).astype(BF16)
    hv = MLA_HEADS * QK_NOPE
    for h in range(MLA_HEADS):
        o = h * Q_PAD
        q_ref[:, o:o + QK_NOPE] = (q[:, o:o + QK_NOPE] * scale).astype(BF16)
        q_ref[:, o + QK_NOPE:o + Q_PAD] = (rope(q[:, o + QK_NOPE:o + Q_PAD]) * scale).astype(BF16)
        k_ref[:, o:o + QK_NOPE] = kv[:, h * QK_NOPE:(h + 1) * QK_NOPE].astype(BF16)
        k_ref[:, o + QK_NOPE:o + Q_PAD] = kpe
        v_ref[:, o:o + V_HEAD] = kv[:, hv + h * V_HEAD:hv + (h + 1) * V_HEAD].astype(BF16)
        v_ref[:, o + V_HEAD:o + Q_PAD] = ones_col


def _ml_up(small, cs, sn, qn, kvn, wq, wkv, seq, tm=256):
    n = small.shape[0]
    per_seq = seq // tm
    row = lambda w: pl.BlockSpec((tm, w), lambda i: (i, 0))
    pos = pl.BlockSpec((tm, LANES), lambda i: (i % per_seq, 0))
    full = lambda shape: pl.BlockSpec(shape, lambda i: (0,) * len(shape))
    hq = MLA_HEADS * Q_PAD
    return pl.pallas_call(
        _ml_up_kernel,
        grid=(n // tm,),
        in_specs=[row(SMALL_COLS), pos, pos, full((1, Q_LORA)), full((1, KV_LORA)),
                  full((Q_LORA, hq)), full((KV_LORA, 2 * MLA_HEADS * V_HEAD))],
        out_specs=[row(hq), row(hq), row(hq)],
        out_shape=[jax.ShapeDtypeStruct((n, hq), BF16)] * 3,
        compiler_params=_params(("parallel",)),
        name="mla_up",
    )(small, cs, sn, qn, kvn, wq, wkv)


def _attn_kernel(q_ref, k_ref, v_ref, o_ref, m_ref, acc_ref, *, tk, parts):
    tq = q_ref.shape[0]
    rows = tq // parts
    nk = k_ref.shape[0] // tk
    nt = (((1,), (1,)), ((), ()))
    sl = [slice(a * rows, (a + 1) * rows) for a in range(parts)]
    lane_tiles = tk // LANES

    def wide(x, reps):
        return jnp.concatenate([x] * reps, axis=1)

    def row_max(s):
        return jnp.broadcast_to(jnp.max(s, axis=1, keepdims=True), (s.shape[0], LANES))

    m_ref[...] = jnp.full_like(m_ref, -jnp.inf)
    acc_ref[...] = jnp.zeros_like(acc_ref)

    def body(j, carry):
        off = pl.multiple_of(j * tk, tk)
        kj = k_ref[pl.ds(off, tk), :]
        vj = v_ref[pl.ds(off, tk), :]
        ss = [lax.dot_general(q_ref[r, :], kj, nt, preferred_element_type=F32) for r in sl]
        for r, s in zip(sl, ss):
            mo = m_ref[r, :]
            mn = jnp.maximum(mo, row_max(s))
            p = jnp.exp2(s - wide(mn, lane_tiles)).astype(BF16)
            acc_ref[r, :] = (acc_ref[r, :] * wide(jnp.exp2(mo - mn), Q_PAD // LANES)
                             + jnp.dot(p, vj, preferred_element_type=F32))
            m_ref[r, :] = mn
        return carry

    lax.fori_loop(0, nk, body, 0)
    acc = acc_ref[...]
    o_ref[...] = (acc[:, :V_HEAD] / acc[:, V_HEAD:V_HEAD + 1]).astype(o_ref.dtype)


def _attention(q, k, v, batch, seq, tq=1024, tk=2048, parts=2):
    n = q.shape[0]
    tq = min(tq, seq)
    tk = min(tk, seq)
    nq = seq // tq
    return pl.pallas_call(
        functools.partial(_attn_kernel, tk=tk, parts=parts),
        grid=(batch, MLA_HEADS, nq),
        in_specs=[
            pl.BlockSpec((tq, Q_PAD), lambda b, h, i: (b * nq + i, h)),
            pl.BlockSpec((seq, Q_PAD), lambda b, h, i: (b, h)),
            pl.BlockSpec((seq, Q_PAD), lambda b, h, i: (b, h)),
        ],
        out_specs=pl.BlockSpec((tq, V_HEAD), lambda b, h, i: (b * nq + i, h)),
        out_shape=jax.ShapeDtypeStruct((n, MLA_HEADS * V_HEAD), BF16),
        scratch_shapes=[pltpu.VMEM((tq, LANES), F32), pltpu.VMEM((tq, Q_PAD), F32)],
        compiler_params=_params(("parallel", "parallel", "parallel")),
        name="mla_flash_attention",
    )(q, k, v)


def _ml_post_kernel(o_ref, g_ref, x_ref, wo_ref, fg_ref, y_ref, *, final):
    g = g_ref[...]
    out = o_ref[...].astype(F32) * (g * _sigmoid(g))
    y = x_ref[...] + _bdot(out, wo_ref[...])
    if final:
        y = _rms(y, fg_ref[...])
    y_ref[...] = y


def _ml_post(o, g, x, w_out, final_g, final, tm=256):
    n = x.shape[0]
    row = lambda w: pl.BlockSpec((tm, w), lambda i: (i, 0))
    full = lambda shape: pl.BlockSpec(shape, lambda i: (0,) * len(shape))
    return pl.pallas_call(
        functools.partial(_ml_post_kernel, final=final),
        grid=(n // tm,),
        in_specs=[row(D_INNER), row(D_INNER), row(D_MODEL), full((D_INNER, D_MODEL)),
                  full((1, D_MODEL))],
        out_specs=row(D_MODEL),
        out_shape=jax.ShapeDtypeStruct((n, D_MODEL), F32),
        compiler_params=_params(("parallel",)),
        name="mla_post",
    )(o, g, x, w_out, final_g)


def _rotate_half_cols(w):
    half = QK_ROPE // 2
    return jnp.concatenate([-w[..., half:], w[..., :half]], axis=-1)


def _prep_rwkv(j, rw_in, rw_w0, rw_w2, rw_a0, rw_a2, rw_kk, rw_ka, rw_rk, rw_lnx_g, rw_lnx_b,
               rw_out):
    e = D_INNER
    w = rw_in[j]
    w_main = w[:, :4 * e].astype(BF16)
    ww = w[:, 4 * e:4 * e + 2 * LORA]
    wa = w[:, 4 * e + 2 * LORA:]
    z = jnp.zeros_like(ww)
    w_small = jnp.concatenate([jnp.concatenate([ww, z], axis=1),
                               jnp.concatenate([z, wa], axis=1)], axis=0).astype(BF16)
    return dict(
        w_main=w_main, w_small=w_small, w0=rw_w0[j], w2=rw_w2[j].astype(BF16), a0=rw_a0[j],
        a2=rw_a2[j].astype(BF16), k_k=rw_kk[j][None], k_a=rw_ka[j][None],
        r_k=rw_rk[j].reshape(1, e), lnx_g=rw_lnx_g[j][None], lnx_b=rw_lnx_b[j][None],
        w_out=rw_out[j].astype(BF16))


def _prep_mla(j, ml_in, ml_qn, ml_kvn, ml_uq, ml_ukv, ml_out):
    w = ml_in[j]
    lo = Q_LORA + KV_LORA
    w_kpe = w[:, lo:lo + QK_ROPE]
    w_small = jnp.concatenate([w[:, :lo], w_kpe, _rotate_half_cols(w_kpe)], axis=1).astype(BF16)
    w_g = w[:, lo + QK_ROPE:].astype(BF16)
    uq = ml_uq[j].reshape(Q_LORA, MLA_HEADS, QK_HEAD)
    pe = uq[..., QK_NOPE:]
    wq = jnp.concatenate([uq[..., :QK_NOPE], pe, _rotate_half_cols(pe)], axis=-1)
    wq = wq.reshape(Q_LORA, MLA_HEADS * Q_PAD).astype(BF16)
    ukv = ml_ukv[j].reshape(KV_LORA, MLA_HEADS, QK_NOPE + V_HEAD)
    wkv = jnp.concatenate([ukv[..., :QK_NOPE].reshape(KV_LORA, -1),
                           ukv[..., QK_NOPE:].reshape(KV_LORA, -1)], axis=1).astype(BF16)
    return dict(w_small=w_small, w_g=w_g, qn=ml_qn[j][None], kvn=ml_kvn[j][None], wq=wq,
                wkv=wkv, w_out=ml_out[j].astype(BF16))


def _rope_tables(seq):
    inv_freq = 1.0 / (ROPE_THETA ** (jnp.arange(0, QK_ROPE, 2, dtype=F32) / QK_ROPE))
    ang = jnp.arange(seq, dtype=F32)[:, None] * inv_freq[None, :]
    z = jnp.zeros((seq, LANES - QK_ROPE), F32)
    cs = jnp.concatenate([jnp.cos(ang), jnp.cos(ang), z], axis=1)
    sn = jnp.concatenate([jnp.sin(ang), jnp.sin(ang), z], axis=1)
    return cs, sn


def _head_indicator():
    e = lax.broadcasted_iota(jnp.int32, (D_INNER, LANES), 0)
    h = lax.broadcasted_iota(jnp.int32, (D_INNER, LANES), 1)
    ind = ((e // RWKV_HEAD) == h).astype(BF16)
    return ind, ind.T


def _trunk(x3, ln_g, final_g, rw, ml):
    batch, seq, _ = x3.shape
    x = x3.reshape(batch * seq, D_MODEL)
    ind, indt = _head_indicator()
    cs, sn = _rope_tables(seq)
    for i in range(DEPTH):
        g = ln_g[i][None]
        if i % 2 == 0:
            p = rw[i // 2]
            xs4, xs2 = _rw_prep(x, g, p["mu"], seq)
            proj = _matmul_streams(xs4, p["w_main"])
            small = _matmul(xs2, p["w_small"])
            yf, yb = _wkv(proj, small, p["w0"], p["w2"], p["a0"], p["a2"], p["k_k"], p["k_a"],
                          batch, seq)
            x = _rw_post(yf, yb, proj, small, x, p["a0"], p["a2"], p["k_a"], p["r_k"],
                         p["lnx_g"], p["lnx_b"], ind, indt, p["w_out"])
        else:
            p = ml[i // 2]
            h = _rmsnorm_bf16(x, g)
            small = _matmul(h, p["w_small"], tn=SMALL_COLS)
            gate = _matmul(h, p["w_g"])
            q, k, v = _ml_up(small, cs, sn, p["qn"], p["kvn"], p["wq"], p["wkv"], seq)
            o = _attention(q, k, v, batch, seq)
            x = _ml_post(o, gate, x, p["w_out"], final_g[None], final=(i == DEPTH - 1))
    return x.reshape(batch, seq, D_MODEL)


def kernel(x_prompt, x_sample, ln_g, final_g, rw_mu, rw_in, rw_w0, rw_w2, rw_a0, rw_a2, rw_kk,
           rw_ka, rw_rk, rw_lnx_g, rw_lnx_b, rw_out, ml_in, ml_qn, ml_kvn, ml_uq, ml_ukv, ml_out):
    rw = []
    for j in range(rw_in.shape[0]):
        p = _prep_rwkv(j, rw_in, rw_w0, rw_w2, rw_a0, rw_a2, rw_kk, rw_ka, rw_rk, rw_lnx_g,
                       rw_lnx_b, rw_out)
        p["mu"] = rw_mu[j]
        rw.append(p)
    ml = [_prep_mla(j, ml_in, ml_qn, ml_kvn, ml_uq, ml_ukv, ml_out) for j in range(ml_in.shape[0])]
    y_prompt = _trunk(x_prompt, ln_g, final_g, rw, ml)
    y_sample = _trunk(x_sample, ln_g, final_g, rw, ml)
    return (y_prompt, y_sample)
```

```python
import functools

import jax
import jax.numpy as jnp
from jax import lax
from jax.experimental import pallas as pl
from jax.experimental.pallas import tpu as pltpu

F32 = jnp.float32
BF16 = jnp.bfloat16

D_MODEL = 1024
D_INNER = 2048
DEPTH = 4
RWKV_HEAD = 64
RWKV_HEADS = D_INNER // RWKV_HEAD
LORA = 64
LN_X_EPS = 64e-5
NORM_EPS = 1e-6
MLA_HEADS = 16
QK_NOPE = 128
QK_ROPE = 64
V_HEAD = 128
QK_HEAD = QK_NOPE + QK_ROPE
Q_LORA = 384
KV_LORA = 256
ROPE_THETA = 10000.0
Q_PAD = 256
SMALL_COLS = Q_LORA + KV_LORA + 2 * QK_ROPE

LANES = 128
SUBLANES = 8
WKV_CHUNK = 64
WKV_HEADS_PER_STEP = 8
WKV_CHUNKS_PER_STEP = 2
INV_BASE = 16
EXP_NEG_HALF = 0.6065306597126334
LOG2_E = 1.4426950408889634
VMEM_LIMIT = 48 * 1024 * 1024

HIGHEST = lax.Precision.HIGHEST


def _bdot(a, b):
    return jnp.dot(a.astype(BF16), b.astype(BF16), preferred_element_type=F32)


def _bdot_nt(a, b):
    return lax.dot_general(a.astype(BF16), b.astype(BF16), (((1,), (1,)), ((), ())),
                           preferred_element_type=F32)


def _bdot_tn(a, b):
    return lax.dot_general(a.astype(BF16), b.astype(BF16), (((0,), (0,)), ((), ())),
                           preferred_element_type=F32)


def _split2(x):
    hi = x.astype(BF16)
    lo = (x - hi.astype(F32)).astype(BF16)
    return hi, lo


def _dot_x2(x, m_bf16):
    hi, lo = _split2(x)
    return (jnp.dot(hi, m_bf16, preferred_element_type=F32)
            + jnp.dot(lo, m_bf16, preferred_element_type=F32))


def _sigmoid(x):
    return 1.0 / (1.0 + jnp.exp(-x))


def _params(sem):
    return pltpu.CompilerParams(dimension_semantics=sem, vmem_limit_bytes=VMEM_LIMIT)


def _rms(x, g):
    return x * lax.rsqrt(jnp.mean(x * x, axis=-1, keepdims=True) + NORM_EPS) * g


def _rw_prep_kernel(x_ref, xp_ref, xn_ref, g_ref, mu_ref, xs4_ref, xs2_ref, *, tm, seq):
    i = pl.program_id(0)
    g = g_ref[...]
    h = _rms(x_ref[...], g)
    hp = _rms(xp_ref[SUBLANES - 1:SUBLANES, :], g)
    hn = _rms(xn_ref[0:1, :], g)
    row0 = i * tm
    hp = jnp.where((row0 % seq) == 0, 0.0, hp)
    hn = jnp.where(((row0 + tm) % seq) == 0, 0.0, hn)
    rid = lax.broadcasted_iota(jnp.int32, (tm, 1), 0)
    prev = jnp.where(rid == 0, hp, pltpu.roll(h, 1, axis=0))
    nxt = jnp.where(rid == tm - 1, hn, pltpu.roll(h, tm - 1, axis=0))
    xx = 0.5 * (prev + nxt) - h
    mu = mu_ref[...]
    for n, s in enumerate((0, 2, 3, 5)):
        xs4_ref[n] = (h + mu[s:s + 1, :] * xx).astype(BF16)
    xs2_ref[:, :D_MODEL] = (h + mu[1:2, :] * xx).astype(BF16)
    xs2_ref[:, D_MODEL:] = (h + mu[4:5, :] * xx).astype(BF16)


def _rw_prep(x, ln_g, mu, seq, tm=256):
    n = x.shape[0]
    nb8 = n // SUBLANES
    per = tm // SUBLANES
    return pl.pallas_call(
        functools.partial(_rw_prep_kernel, tm=tm, seq=seq),
        grid=(n // tm,),
        in_specs=[
            pl.BlockSpec((tm, D_MODEL), lambda i: (i, 0)),
            pl.BlockSpec((SUBLANES, D_MODEL), lambda i: (jnp.maximum(i * per - 1, 0), 0)),
            pl.BlockSpec((SUBLANES, D_MODEL), lambda i: (jnp.minimum((i + 1) * per, nb8 - 1), 0)),
            pl.BlockSpec((1, D_MODEL), lambda i: (0, 0)),
            pl.BlockSpec((6, D_MODEL), lambda i: (0, 0)),
        ],
        out_specs=[
            pl.BlockSpec((4, tm, D_MODEL), lambda i: (0, i, 0)),
            pl.BlockSpec((tm, 2 * D_MODEL), lambda i: (i, 0)),
        ],
        out_shape=[
            jax.ShapeDtypeStruct((4, n, D_MODEL), BF16),
            jax.ShapeDtypeStruct((n, 2 * D_MODEL), BF16),
        ],
        compiler_params=_params(("parallel",)),
        name="rw_prep",
    )(x, x, x, ln_g, mu)


def _norm_kernel(x_ref, g_ref, o_ref):
    o_ref[...] = _rms(x_ref[...], g_ref[...]).astype(o_ref.dtype)


def _rmsnorm_bf16(x, g, tm=512):
    n = x.shape[0]
    return pl.pallas_call(
        _norm_kernel,
        grid=(n // tm,),
        in_specs=[pl.BlockSpec((tm, D_MODEL), lambda i: (i, 0)),
                  pl.BlockSpec((1, D_MODEL), lambda i: (0, 0))],
        out_specs=pl.BlockSpec((tm, D_MODEL), lambda i: (i, 0)),
        out_shape=jax.ShapeDtypeStruct((n, D_MODEL), BF16),
        compiler_params=_params(("parallel",)),
        name="rmsnorm",
    )(x, g)


def _mm_kernel(x_ref, w_ref, o_ref):
    o_ref[...] = jnp.dot(x_ref[...], w_ref[...], preferred_element_type=F32).astype(o_ref.dtype)


def _matmul(x, w, tm=512, tn=512):
    n, k = x.shape
    m = w.shape[1]
    tn = min(tn, m)
    return pl.pallas_call(
        _mm_kernel,
        grid=(n // tm, m // tn),
        in_specs=[pl.BlockSpec((tm, k), lambda i, j: (i, 0)),
                  pl.BlockSpec((k, tn), lambda i, j: (0, j))],
        out_specs=pl.BlockSpec((tm, tn), lambda i, j: (i, j)),
        out_shape=jax.ShapeDtypeStruct((n, m), F32),
        compiler_params=_params(("parallel", "parallel")),
        name="matmul",
    )(x, w)


def _matmul_streams(xs, w, tm=1024, tn=512):
    ns, n, k = xs.shape
    per = D_INNER // tn
    return pl.pallas_call(
        _mm_kernel,
        grid=(n // tm, ns, per),
        in_specs=[pl.BlockSpec((None, tm, k), lambda i, s, j: (s, i, 0)),
                  pl.BlockSpec((k, tn), lambda i, s, j: (0, s * per + j))],
        out_specs=pl.BlockSpec((tm, tn), lambda i, s, j: (i, s * per + j)),
        out_shape=jax.ShapeDtypeStruct((n, ns * D_INNER), BF16),
        compiler_params=_params(("parallel", "parallel", "parallel")),
        name="matmul_streams",
    )(xs, w)


def _unit_tri_inverses(mats, row, col, c):
    eye = (row == col).astype(F32)
    same = (row // INV_BASE) == (col // INV_BASE)
    ps = [jnp.where(same, a, 0.0) for a in mats]
    xs = [eye + p for p in ps]
    span = 2
    while span < INV_BASE:
        ps = [_bdot(p, p) for p in ps]
        xs = [x + _bdot(x, p) for x, p in zip(xs, ps)]
        span *= 2
    blk = INV_BASE
    while blk < c:
        pair = ((row // (2 * blk)) == (col // (2 * blk))) & jnp.logical_not(
            (row // blk) == (col // blk))
        ts = [_bdot(jnp.where(pair, a, 0.0), x) for a, x in zip(mats, xs)]
        xs = [x + _bdot(x, t) for x, t in zip(xs, ts)]
        blk *= 2
    return xs


def _wkv_kernel(rf_ref, kf_ref, vf_ref, sf_ref, rb_ref, kb_ref, vb_ref, sb_ref,
                w0_ref, w2_ref, a0_ref, a2_ref, kk_ref, ka_ref,
                yf_ref, yb_ref, state_ref, *, chunk, heads, sub):
    c = chunk
    n = RWKV_HEAD
    width = heads * n
    rows_blk = sub * c

    @pl.when(pl.program_id(2) == 0)
    def _():
        state_ref[...] = jnp.zeros_like(state_ref)

    pc = 2 * c
    row = lax.broadcasted_iota(jnp.int32, (pc, pc), 0)
    col = lax.broadcasted_iota(jnp.int32, (pc, pc), 1)
    same_head = (row // c) == (col // c)
    lane = lax.broadcasted_iota(jnp.int32, (c, LANES), 1)
    head_a = lane < n

    def stack(x):
        zero = jnp.zeros_like(x)
        return jnp.concatenate([jnp.where(head_a, x, zero), jnp.where(head_a, zero, x)], axis=0)

    lrow = lax.broadcasted_iota(jnp.int32, (width, width), 0)
    lcol = lax.broadcasted_iota(jnp.int32, (width, width), 1)
    seg_ones = ((lrow // n) == (lcol // n)).astype(BF16)
    brow = lax.broadcasted_iota(jnp.int32, (rows_blk, rows_blk), 0)
    bcol = lax.broadcasted_iota(jnp.int32, (rows_blk, rows_blk), 1)
    same_chunk = (brow // c) == (bcol // c)
    ones_c = jnp.ones((c, LANES), BF16)
    tn = (((0,), (0,)), ((), ()))

    units = []
    dirs = ((rf_ref, kf_ref, vf_ref, sf_ref), (rb_ref, kb_ref, vb_ref, sb_ref))
    for d, (r_ref, k_ref, v_ref, s_ref) in enumerate(dirs):
        strict = same_head & ((row > col) if d == 0 else (row < col))
        incl = same_head & ((row >= col) if d == 0 else (row <= col))
        r = r_ref[...].astype(F32)
        k = k_ref[...].astype(F32)
        vb = v_ref[...]
        sm = s_ref[...]
        wl = sm[:, d * LORA:(d + 1) * LORA]
        al = sm[:, (2 + d) * LORA:(3 + d) * LORA]
        w_pre = w0_ref[d:d + 1, :] + _bdot(jnp.tanh(wl), w2_ref[d])
        logw = -EXP_NEG_HALF * _sigmoid(w_pre)
        a = _sigmoid(a0_ref[d:d + 1, :] + _bdot(al, a2_ref[d]))
        kk = k * kk_ref[...]
        ss = jnp.dot((kk * kk).astype(BF16), seg_ones, preferred_element_type=F32)
        kk = kk / jnp.maximum(jnp.sqrt(ss), 1e-12)
        kdir = k * (1.0 + (a - 1.0) * ka_ref[...])
        b = kk * a
        tri = (same_chunk & ((brow >= bcol) if d == 0 else (brow <= bcol))).astype(BF16)
        l_hi, l_lo = _split2(logw)
        cl = (jnp.dot(tri, l_hi, preferred_element_type=F32)
              + jnp.dot(tri, l_lo, preferred_element_type=F32))
        ends = []
        for ci in range(sub):
            last = ci * c + (c - 1 if d == 0 else 0)
            ends.append(jnp.broadcast_to(cl[last:last + 1, :], (c, width)))
        cl_end = jnp.concatenate(ends, axis=0)
        e_in = jnp.exp(cl)
        e_ex = jnp.exp(cl - logw)
        e_neg = jnp.exp(-cl)
        e_end = jnp.exp(cl_end - cl)
        rt = (r * e_in).astype(BF16)
        at = (-kk * e_ex).astype(BF16)
        bt = (b * e_neg).astype(BF16)
        kt = (kdir * e_neg).astype(BF16)
        bh = (b * e_end).astype(BF16)
        kh = (kdir * e_end).astype(BF16)
        for ci in range(sub):
            rs = slice(ci * c, (ci + 1) * c)
            w_end_t = jnp.exp(lax.dot_general(l_hi[rs], ones_c, tn, preferred_element_type=F32)
                              + lax.dot_general(l_lo[rs], ones_c, tn, preferred_element_type=F32))
            for p in range(heads // 2):
                ls = slice(p * LANES, (p + 1) * LANES)
                at_s, rt_s = stack(at[rs, ls]), stack(rt[rs, ls])
                units.append(dict(
                    d=d, ci=ci, p=p, strict=strict, incl=incl, at=at_s, rt=rt_s,
                    v=stack(vb[rs, ls]),
                    lhs=jnp.concatenate([at_s, rt_s], axis=0),
                    rhs=jnp.concatenate([stack(bt[rs, ls]), stack(kt[rs, ls])], axis=0),
                    end=jnp.concatenate([stack(bh[rs, ls]), stack(kh[rs, ls])], axis=0),
                    w_end=w_end_t[ls, :]))

    gs = [_bdot_nt(u["lhs"], u["rhs"]) for u in units]
    a_ab = [jnp.where(u["strict"], g[:pc, :pc], 0.0) for u, g in zip(units, gs)]
    a_ak = [jnp.where(u["strict"], g[:pc, pc:], 0.0).astype(BF16) for u, g in zip(units, gs)]
    a_r = [jnp.concatenate([jnp.where(u["incl"], g[pc:, :pc], 0.0),
                            jnp.where(u["incl"], g[pc:, pc:], 0.0)], axis=1).astype(BF16)
           for u, g in zip(units, gs)]
    pv = [_bdot(a, u["v"]) for a, u in zip(a_ak, units)]
    tinv = _unit_tri_inverses(a_ab, row, col, c)
    qs = [_bdot(t, jnp.concatenate([u["at"], p.astype(BF16)], axis=1))
          for t, u, p in zip(tinv, units, pv)]
    zs = [jnp.concatenate([q.astype(BF16),
                           jnp.concatenate([jnp.zeros((pc, LANES), BF16), u["v"]], axis=1)],
                          axis=0)
          for q, u in zip(qs, units)]
    rys = [_bdot(a, z) for a, z in zip(a_r, zs)]
    mss = [_bdot_tn(u["end"], z) for u, z in zip(units, zs)]
    by_key = {(u["d"], u["ci"], u["p"]): i for i, u in enumerate(units)}
    pairs = heads // 2
    for d, y_ref in enumerate((yf_ref, yb_ref)):
        state = [state_ref[d, p] for p in range(pairs)]
        order = range(sub) if d == 0 else range(sub - 1, -1, -1)
        for ci in order:
            idx = [by_key[(d, ci, p)] for p in range(pairs)]
            ys = [_bdot(units[i]["rt"].astype(F32) + rys[i][:, :LANES], s) + rys[i][:, LANES:]
                  for i, s in zip(idx, state)]
            state = [_bdot(mss[i][:, :LANES], s) + units[i]["w_end"] * s + mss[i][:, LANES:]
                     for i, s in zip(idx, state)]
            y_ref[ci * c:(ci + 1) * c, :] = jnp.concatenate(
                [y[:c] + y[c:] for y in ys], axis=1).astype(y_ref.dtype)
        for p in range(pairs):
            state_ref[d, p] = state[p]


def _wkv(proj, small, w0, w2, a0, a2, k_k, k_a, batch, seq):
    n_tok = proj.shape[0]
    hg = WKV_HEADS_PER_STEP
    width = hg * RWKV_HEAD
    rows = WKV_CHUNK * WKV_CHUNKS_PER_STEP
    nb = seq // rows
    groups = D_INNER // width

    def fwd(s):
        return lambda b, g, t: (b * nb + t, s * groups + g)

    def bwd(s):
        return lambda b, g, t: (b * nb + (nb - 1 - t), s * groups + g)

    blk = lambda f: pl.BlockSpec((rows, width), f)
    par = lambda shape: pl.BlockSpec(shape, lambda b, g, t: (0,) * (len(shape) - 1) + (g,))
    return pl.pallas_call(
        functools.partial(_wkv_kernel, chunk=WKV_CHUNK, heads=hg, sub=WKV_CHUNKS_PER_STEP),
        grid=(batch, groups, nb),
        in_specs=[
            blk(fwd(0)), blk(fwd(1)), blk(fwd(2)),
            pl.BlockSpec((rows, 4 * LORA), lambda b, g, t: (b * nb + t, 0)),
            blk(bwd(0)), blk(bwd(1)), blk(bwd(2)),
            pl.BlockSpec((rows, 4 * LORA), lambda b, g, t: (b * nb + (nb - 1 - t), 0)),
            par((2, width)), par((2, LORA, width)), par((2, width)), par((2, LORA, width)),
            par((1, width)), par((1, width)),
        ],
        out_specs=[
            pl.BlockSpec((rows, width), lambda b, g, t: (b * nb + t, g)),
            pl.BlockSpec((rows, width), lambda b, g, t: (b * nb + (nb - 1 - t), g)),
        ],
        out_shape=[jax.ShapeDtypeStruct((n_tok, D_INNER), BF16),
                   jax.ShapeDtypeStruct((n_tok, D_INNER), BF16)],
        scratch_shapes=[pltpu.VMEM((2, hg // 2, LANES, LANES), F32)],
        compiler_params=_params(("parallel", "parallel", "arbitrary")),
        name="wkv7_chunked",
    )(proj, proj, proj, small, proj, proj, proj, small, w0, w2, a0, a2, k_k, k_a)


def _rw_post_kernel(yf_ref, yb_ref, r_ref, k_ref, v_ref, g_ref, s_ref, x_ref,
                    a0_ref, a2_ref, ka_ref, rk_ref, lg_ref, lb_ref, ind_ref, indt_ref, wo_ref,
                    o_ref):
    ind = ind_ref[...]
    indt = indt_ref[...]

    y = yf_ref[...].astype(F32) + yb_ref[...].astype(F32)
    mean = _dot_x2(_dot_x2(y, ind) * (1.0 / RWKV_HEAD), indt)
    yc = y - mean
    var = _bdot(yc * yc, ind) * (1.0 / RWKV_HEAD)
    rstd = _bdot(lax.rsqrt(var + LN_X_EPS), indt)
    yn = yc * rstd * lg_ref[...] + lb_ref[...]
    sm = s_ref[...]
    a_sum = (_sigmoid(a0_ref[0:1, :] + _bdot(sm[:, 2 * LORA:3 * LORA], a2_ref[0]))
             + _sigmoid(a0_ref[1:2, :] + _bdot(sm[:, 3 * LORA:4 * LORA], a2_ref[1])))
    k_sum = k_ref[...].astype(F32) * (2.0 + (a_sum - 2.0) * ka_ref[...])
    coef = _bdot(_bdot(r_ref[...].astype(F32) * k_sum * rk_ref[...], ind), indt)
    g = g_ref[...].astype(F32)
    out = (yn + coef * v_ref[...].astype(F32)) * (g * _sigmoid(g))
    o_ref[...] = x_ref[...] + _bdot(out, wo_ref[...])


def _rw_post(yf, yb, proj, small, x, a0, a2, k_a, r_k, lnx_g, lnx_b, ind, indt, w_out, tm=256):
    n = x.shape[0]
    e = D_INNER
    row = lambda w, j: pl.BlockSpec((tm, w), lambda i: (i, j))
    full = lambda shape: pl.BlockSpec(shape, lambda i: (0,) * len(shape))
    return pl.pallas_call(
        _rw_post_kernel,
        grid=(n // tm,),
        in_specs=[
            row(e, 0), row(e, 0),
            row(e, 0), row(e, 1), row(e, 2), row(e, 3),
            row(4 * LORA, 0), row(D_MODEL, 0),
            full((2, e)), full((2, LORA, e)), full((1, e)), full((1, e)), full((1, e)),
            full((1, e)), full((e, LANES)), full((LANES, e)), full((e, D_MODEL)),
        ],
        out_specs=row(D_MODEL, 0),
        out_shape=jax.ShapeDtypeStruct((n, D_MODEL), F32),
        compiler_params=_params(("parallel",)),
        name="rw_post",
    )(yf, yb, proj, proj, proj, proj, small, x, a0, a2, k_a, r_k, lnx_g, lnx_b, ind, indt, w_out)


def _ml_up_kernel(s_ref, cs_ref, sn_ref, qn_ref, kvn_ref, wq_ref, wkv_ref,
                  q_ref, k_ref, v_ref):
    sm = s_ref[...]
    cs = cs_ref[...]
    sn = sn_ref[...]
    scale = QK_HEAD ** -0.5 * LOG2_E

    def rms(x, g):
        return x * lax.rsqrt(jnp.mean(x * x, axis=-1, keepdims=True) + NORM_EPS) * g

    def rope(x2):
        return x2 * cs + pltpu.roll(x2, QK_ROPE, axis=1) * sn

    cq = rms(sm[:, :Q_LORA], qn_ref[...])
    ckv = rms(sm[:, Q_LORA:Q_LORA + KV_LORA], kvn_ref[...])
    q = _bdot(cq, wq_ref[...])
    kv = _bdot(ckv, wkv_ref[...])
    kpe = rope(sm[:, Q_LORA + KV_LORA:]).astype(BF16)
    lane = lax.broadcasted_iota(jnp.int32, kpe.shape, 1)
    ones_col = (lane == 0).astype(BF16)
    hv = MLA_HEADS * QK_NOPE
    for h in range(MLA_HEADS):
        o = h * Q_PAD
        q_ref[:, o:o + QK_NOPE] = (q[:, o:o + QK_NOPE] * scale).astype(BF16)
        q_ref[:, o + QK_NOPE:o + Q_PAD] = (rope(q[:, o + QK_NOPE:o + Q_PAD]) * scale).astype(BF16)
        k_ref[:, o:o + QK_NOPE] = kv[:, h * QK_NOPE:(h + 1) * QK_NOPE].astype(BF16)
        k_ref[:, o + QK_NOPE:o + Q_PAD] = kpe
        v_ref[:, o:o + V_HEAD] = kv[:, hv + h * V_HEAD:hv + (h + 1) * V_HEAD].astype(BF16)
        v_ref[:, o + V_HEAD:o + Q_PAD] = ones_col


def _ml_up(small, cs, sn, qn, kvn, wq, wkv, seq, tm=256):
    n = small.shape[0]
    per_seq = seq // tm
    row = lambda w: pl.BlockSpec((tm, w), lambda i: (i, 0))
    pos = pl.BlockSpec((tm, LANES), lambda i: (i % per_seq, 0))
    full = lambda shape: pl.BlockSpec(shape, lambda i: (0,) * len(shape))
    hq = MLA_HEADS * Q_PAD
    return pl.pallas_call(
        _ml_up_kernel,
        grid=(n // tm,),
        in_specs=[row(SMALL_COLS), pos, pos, full((1, Q_LORA)), full((1, KV_LORA)),
                  full((Q_LORA, hq)), full((KV_LORA, 2 * MLA_HEADS * V_HEAD))],
        out_specs=[row(hq), row(hq), row(hq)],
        out_shape=[jax.ShapeDtypeStruct((n, hq), BF16)] * 3,
        compiler_params=_params(("parallel",)),
        name="mla_up",
    )(small, cs, sn, qn, kvn, wq, wkv)


def _attn_kernel(q_ref, k_ref, v_ref, o_ref, m_ref, acc_ref, *, tk, parts):
    tq = q_ref.shape[0]
    rows = tq // parts
    nk = k_ref.shape[0] // tk
    nt = (((1,), (1,)), ((), ()))
    sl = [slice(a * rows, (a + 1) * rows) for a in range(parts)]
    lane_tiles = tk // LANES

    def wide(x, reps):
        return jnp.concatenate([x] * reps, axis=1)

    def row_max(s):
        return jnp.broadcast_to(jnp.max(s, axis=1, keepdims=True), (s.shape[0], LANES))

    m_ref[...] = jnp.full_like(m_ref, -jnp.inf)
    acc_ref[...] = jnp.zeros_like(acc_ref)

    def body(j, carry):
        off = pl.multiple_of(j * tk, tk)
        kj = k_ref[pl.ds(off, tk), :]
        vj = v_ref[pl.ds(off, tk), :]
        ss = [lax.dot_general(q_ref[r, :], kj, nt, preferred_element_type=F32) for r in sl]
        for r, s in zip(sl, ss):
            mo = m_ref[r, :]
            mn = jnp.maximum(mo, row_max(s))
            p = jnp.exp2(s - wide(mn, lane_tiles)).astype(BF16)
            acc_ref[r, :] = (acc_ref[r, :] * wide(jnp.exp2(mo - mn), Q_PAD // LANES)
                             + jnp.dot(p, vj, preferred_element_type=F32))
            m_ref[r, :] = mn
        return carry

    lax.fori_loop(0, nk, body, 0)
    acc = acc_ref[...]
    o_ref[...] = (acc[:, :V_HEAD] / acc[:, V_HEAD:V_HEAD + 1]).astype(o_ref.dtype)


def _attention(q, k, v, batch, seq, tq=1024, tk=2048, parts=2):
    n = q.shape[0]
    tq = min(tq, seq)
    tk = min(tk, seq)
    nq = seq // tq
    return pl.pallas_call(
        functools.partial(_attn_kernel, tk=tk, parts=parts),
        grid=(batch, MLA_HEADS, nq),
        in_specs=[
            pl.BlockSpec((tq, Q_PAD), lambda b, h, i: (b * nq + i, h)),
            pl.BlockSpec((seq, Q_PAD), lambda b, h, i: (b, h)),
            pl.BlockSpec((seq, Q_PAD), lambda b, h, i: (b, h)),
        ],
        out_specs=pl.BlockSpec((tq, V_HEAD), lambda b, h, i: (b * nq + i, h)),
        out_shape=jax.ShapeDtypeStruct((n, MLA_HEADS * V_HEAD), BF16),
        scratch_shapes=[pltpu.VMEM((tq, LANES), F32), pltpu.VMEM((tq, Q_PAD), F32)],
        compiler_params=_params(("parallel", "parallel", "parallel")),
        name="mla_flash_attention",
    )(q, k, v)


def _ml_post_kernel(o_ref, g_ref, x_ref, wo_ref, fg_ref, y_ref, *, final):
    g = g_ref[...]
    out = o_ref[...].astype(F32) * (g * _sigmoid(g))
    y = x_ref[...] + _bdot(out, wo_ref[...])
    if final:
        y = _rms(y, fg_ref[...])
    y_ref[...] = y


def _ml_post(o, g, x, w_out, final_g, final, tm=256):
    n = x.shape[0]
    row = lambda w: pl.BlockSpec((tm, w), lambda i: (i, 0))
    full = lambda shape: pl.BlockSpec(shape, lambda i: (0,) * len(shape))
    return pl.pallas_call(
        functools.partial(_ml_post_kernel, final=final),
        grid=(n // tm,),
        in_specs=[row(D_INNER), row(D_INNER), row(D_MODEL), full((D_INNER, D_MODEL)),
                  full((1, D_MODEL))],
        out_specs=row(D_MODEL),
        out_shape=jax.ShapeDtypeStruct((n, D_MODEL), F32),
        compiler_params=_params(("parallel",)),
        name="mla_post",
    )(o, g, x, w_out, final_g)


def _rotate_half_cols(w):
    half = QK_ROPE // 2
    return jnp.concatenate([-w[..., half:], w[..., :half]], axis=-1)


def _prep_rwkv(j, rw_in, rw_w0, rw_w2, rw_a0, rw_a2, rw_kk, rw_ka, rw_rk, rw_lnx_g, rw_lnx_b,
               rw_out):
    e = D_INNER
    w = rw_in[j]
    w_main = w[:, :4 * e].astype(BF16)
    ww = w[:, 4 * e:4 * e + 2 * LORA]
    wa = w[:, 4 * e + 2 * LORA:]
    z = jnp.zeros_like(ww)
    w_small = jnp.concatenate([jnp.concatenate([ww, z], axis=1),
                               jnp.concatenate([z, wa], axis=1)], axis=0).astype(BF16)
    return dict(
        w_main=w_main, w_small=w_small, w0=rw_w0[j], w2=rw_w2[j].astype(BF16), a0=rw_a0[j],
        a2=rw_a2[j].astype(BF16), k_k=rw_kk[j][None], k_a=rw_ka[j][None],
        r_k=rw_rk[j].reshape(1, e), lnx_g=rw_lnx_g[j][None], lnx_b=rw_lnx_b[j][None],
        w_out=rw_out[j].astype(BF16))


def _prep_mla(j, ml_in, ml_qn, ml_kvn, ml_uq, ml_ukv, ml_out):
    w = ml_in[j]
    lo = Q_LORA + KV_LORA
    w_kpe = w[:, lo:lo + QK_ROPE]
    w_small = jnp.concatenate([w[:, :lo], w_kpe, _rotate_half_cols(w_kpe)], axis=1).astype(BF16)
    w_g = w[:, lo + QK_ROPE:].astype(BF16)
    uq = ml_uq[j].reshape(Q_LORA, MLA_HEADS, QK_HEAD)
    pe = uq[..., QK_NOPE:]
    wq = jnp.concatenate([uq[..., :QK_NOPE], pe, _rotate_half_cols(pe)], axis=-1)
    wq = wq.reshape(Q_LORA, MLA_HEADS * Q_PAD).astype(BF16)
    ukv = ml_ukv[j].reshape(KV_LORA, MLA_HEADS, QK_NOPE + V_HEAD)
    wkv = jnp.concatenate([ukv[..., :QK_NOPE].reshape(KV_LORA, -1),
                           ukv[..., QK_NOPE:].reshape(KV_LORA, -1)], axis=1).astype(BF16)
    return dict(w_small=w_small, w_g=w_g, qn=ml_qn[j][None], kvn=ml_kvn[j][None], wq=wq,
                wkv=wkv, w_out=ml_out[j].astype(BF16))


def _rope_tables(seq):
    inv_freq = 1.0 / (ROPE_THETA ** (jnp.arange(0, QK_ROPE, 2, dtype=F32) / QK_ROPE))
    ang = jnp.arange(seq, dtype=F32)[:, None] * inv_freq[None, :]
    z = jnp.zeros((seq, LANES - QK_ROPE), F32)
    cs = jnp.concatenate([jnp.cos(ang), jnp.cos(ang), z], axis=1)
    sn = jnp.concatenate([jnp.sin(ang), jnp.sin(ang), z], axis=1)
    return cs, sn


def _head_indicator():
    e = lax.broadcasted_iota(jnp.int32, (D_INNER, LANES), 0)
    h = lax.broadcasted_iota(jnp.int32, (D_INNER, LANES), 1)
    ind = ((e // RWKV_HEAD) == h).astype(BF16)
    return ind, ind.T


def _trunk(x3, ln_g, final_g, rw, ml):
    batch, seq, _ = x3.shape
    x = x3.reshape(batch * seq, D_MODEL)
    ind, indt = _head_indicator()
    cs, sn = _rope_tables(seq)
    for i in range(DEPTH):
        g = ln_g[i][None]
        if i % 2 == 0:
            p = rw[i // 2]
            xs4, xs2 = _rw_prep(x, g, p["mu"], seq)
            proj = _matmul_streams(xs4, p["w_main"])
            small = _matmul(xs2, p["w_small"])
            yf, yb = _wkv(proj, small, p["w0"], p["w2"], p["a0"], p["a2"], p["k_k"], p["k_a"],
                          batch, seq)
            x = _rw_post(yf, yb, proj, small, x, p["a0"], p["a2"], p["k_a"], p["r_k"],
                         p["lnx_g"], p["lnx_b"], ind, indt, p["w_out"])
        else:
            p = ml[i // 2]
            h = _rmsnorm_bf16(x, g)
            small = _matmul(h, p["w_small"], tn=SMALL_COLS)
            gate = _matmul(h, p["w_g"])
            q, k, v = _ml_up(small, cs, sn, p["qn"], p["kvn"], p["wq"], p["wkv"], seq)
            o = _attention(q, k, v, batch, seq)
            x = _ml_post(o, gate, x, p["w_out"], final_g[None], final=(i == DEPTH - 1))
    return x.reshape(batch, seq, D_MODEL)


def kernel(x_prompt, x_sample, ln_g, final_g, rw_mu, rw_in, rw_w0, rw_w2, rw_a0, rw_a2, rw_kk,
           rw_ka, rw_rk, rw_lnx_g, rw_lnx_b, rw_out, ml_in, ml_qn, ml_kvn, ml_uq, ml_ukv, ml_out):
    rw = []
    for j in range(rw_in.shape[0]):
        p = _prep_rwkv(j, rw_in, rw_w0, rw_w2, rw_a0, rw_a2, rw_kk, rw_ka, rw_rk, rw_lnx_g,
                       rw_lnx_b, rw_out)
        p["mu"] = rw_mu[j]
        rw.append(p)
    ml = [_prep_mla(j, ml_in, ml_qn, ml_kvn, ml_uq, ml_ukv, ml_out) for j in range(ml_in.shape[0])]
    y_prompt = _trunk(x_prompt, ln_g, final_g, rw, ml)
    y_sample = _trunk(x_sample, ln_g, final_g, rw, ml)
    return (y_prompt, y_sample)
```

```python
import functools

import jax
import jax.numpy as jnp
from jax import lax
from jax.experimental import pallas as pl
from jax.experimental.pallas import tpu as pltpu

F32 = jnp.float32
BF16 = jnp.bfloat16

D_MODEL = 1024
D_INNER = 2048
DEPTH = 4
RWKV_HEAD = 64
RWKV_HEADS = D_INNER // RWKV_HEAD
LORA = 64
LN_X_EPS = 64e-5
NORM_EPS = 1e-6
MLA_HEADS = 16
QK_NOPE = 128
QK_ROPE = 64
V_HEAD = 128
QK_HEAD = QK_NOPE + QK_ROPE
Q_LORA = 384
KV_LORA = 256
ROPE_THETA = 10000.0
Q_PAD = 256
SMALL_COLS = Q_LORA + KV_LORA + 2 * QK_ROPE

LANES = 128
SUBLANES = 8
WKV_CHUNK = 64
WKV_HEADS_PER_STEP = 8
WKV_CHUNKS_PER_STEP = 4
INV_BASE = 16
EXP_NEG_HALF = 0.6065306597126334
LOG2_E = 1.4426950408889634
VMEM_LIMIT = 48 * 1024 * 1024

HIGHEST = lax.Precision.HIGHEST


def _bdot(a, b):
    return jnp.dot(a.astype(BF16), b.astype(BF16), preferred_element_type=F32)


def _bdot_nt(a, b):
    return lax.dot_general(a.astype(BF16), b.astype(BF16), (((1,), (1,)), ((), ())),
                           preferred_element_type=F32)


def _bdot_tn(a, b):
    return lax.dot_general(a.astype(BF16), b.astype(BF16), (((0,), (0,)), ((), ())),
                           preferred_element_type=F32)


def _split2(x):
    hi = x.astype(BF16)
    lo = (x - hi.astype(F32)).astype(BF16)
    return hi, lo


def _dot_x2(x, m_bf16):
    hi, lo = _split2(x)
    return (jnp.dot(hi, m_bf16, preferred_element_type=F32)
            + jnp.dot(lo, m_bf16, preferred_element_type=F32))


def _sigmoid(x):
    return 1.0 / (1.0 + jnp.exp(-x))


def _params(sem):
    return pltpu.CompilerParams(dimension_semantics=sem, vmem_limit_bytes=VMEM_LIMIT)


def _rms(x, g):
    return x * lax.rsqrt(jnp.mean(x * x, axis=-1, keepdims=True) + NORM_EPS) * g


def _rw_prep_kernel(x_ref, xp_ref, xn_ref, g_ref, mu_ref, xs4_ref, xs2_ref, *, tm, seq):
    i = pl.program_id(0)
    g = g_ref[...]
    h = _rms(x_ref[...], g)
    hp = _rms(xp_ref[SUBLANES - 1:SUBLANES, :], g)
    hn = _rms(xn_ref[0:1, :], g)
    row0 = i * tm
    hp = jnp.where((row0 % seq) == 0, 0.0, hp)
    hn = jnp.where(((row0 + tm) % seq) == 0, 0.0, hn)
    rid = lax.broadcasted_iota(jnp.int32, (tm, 1), 0)
    prev = jnp.where(rid == 0, hp, pltpu.roll(h, 1, axis=0))
    nxt = jnp.where(rid == tm - 1, hn, pltpu.roll(h, tm - 1, axis=0))
    xx = 0.5 * (prev + nxt) - h
    mu = mu_ref[...]
    for n, s in enumerate((0, 2, 3, 5)):
        xs4_ref[n] = (h + mu[s:s + 1, :] * xx).astype(BF16)
    xs2_ref[:, :D_MODEL] = (h + mu[1:2, :] * xx).astype(BF16)
    xs2_ref[:, D_MODEL:] = (h + mu[4:5, :] * xx).astype(BF16)


def _rw_prep(x, ln_g, mu, seq, tm=256):
    n = x.shape[0]
    nb8 = n // SUBLANES
    per = tm // SUBLANES
    return pl.pallas_call(
        functools.partial(_rw_prep_kernel, tm=tm, seq=seq),
        grid=(n // tm,),
        in_specs=[
            pl.BlockSpec((tm, D_MODEL), lambda i: (i, 0)),
            pl.BlockSpec((SUBLANES, D_MODEL), lambda i: (jnp.maximum(i * per - 1, 0), 0)),
            pl.BlockSpec((SUBLANES, D_MODEL), lambda i: (jnp.minimum((i + 1) * per, nb8 - 1), 0)),
            pl.BlockSpec((1, D_MODEL), lambda i: (0, 0)),
            pl.BlockSpec((6, D_MODEL), lambda i: (0, 0)),
        ],
        out_specs=[
            pl.BlockSpec((4, tm, D_MODEL), lambda i: (0, i, 0)),
            pl.BlockSpec((tm, 2 * D_MODEL), lambda i: (i, 0)),
        ],
        out_shape=[
            jax.ShapeDtypeStruct((4, n, D_MODEL), BF16),
            jax.ShapeDtypeStruct((n, 2 * D_MODEL), BF16),
        ],
        compiler_params=_params(("parallel",)),
        name="rw_prep",
    )(x, x, x, ln_g, mu)


def _norm_kernel(x_ref, g_ref, o_ref):
    o_ref[...] = _rms(x_ref[...], g_ref[...]).astype(o_ref.dtype)


def _rmsnorm_bf16(x, g, tm=512):
    n = x.shape[0]
    return pl.pallas_call(
        _norm_kernel,
        grid=(n // tm,),
        in_specs=[pl.BlockSpec((tm, D_MODEL), lambda i: (i, 0)),
                  pl.BlockSpec((1, D_MODEL), lambda i: (0, 0))],
        out_specs=pl.BlockSpec((tm, D_MODEL), lambda i: (i, 0)),
        out_shape=jax.ShapeDtypeStruct((n, D_MODEL), BF16),
        compiler_params=_params(("parallel",)),
        name="rmsnorm",
    )(x, g)


def _mm_kernel(x_ref, w_ref, o_ref):
    o_ref[...] = jnp.dot(x_ref[...], w_ref[...], preferred_element_type=F32).astype(o_ref.dtype)


def _matmul(x, w, out_dtype=F32, tm=1024, tn=512):
    n, k = x.shape
    m = w.shape[1]
    tn = min(tn, m)
    return pl.pallas_call(
        _mm_kernel,
        grid=(n // tm, m // tn),
        in_specs=[pl.BlockSpec((tm, k), lambda i, j: (i, 0)),
                  pl.BlockSpec((k, tn), lambda i, j: (0, j))],
        out_specs=pl.BlockSpec((tm, tn), lambda i, j: (i, j)),
        out_shape=jax.ShapeDtypeStruct((n, m), out_dtype),
        compiler_params=_params(("parallel", "parallel")),
        name="matmul",
    )(x, w)


def _matmul_streams(xs, w, tm=2048, tn=512):
    ns, n, k = xs.shape
    per = D_INNER // tn
    return pl.pallas_call(
        _mm_kernel,
        grid=(n // tm, ns, per),
        in_specs=[pl.BlockSpec((None, tm, k), lambda i, s, j: (s, i, 0)),
                  pl.BlockSpec((k, tn), lambda i, s, j: (0, s * per + j))],
        out_specs=pl.BlockSpec((tm, tn), lambda i, s, j: (i, s * per + j)),
        out_shape=jax.ShapeDtypeStruct((n, ns * D_INNER), BF16),
        compiler_params=_params(("parallel", "parallel", "parallel")),
        name="matmul_streams",
    )(xs, w)


def _unit_tri_inverses(mats, row, col, c):
    eye = (row == col).astype(F32)
    same = (row // INV_BASE) == (col // INV_BASE)
    ps = [jnp.where(same, a, 0.0) for a in mats]
    xs = [eye + p for p in ps]
    span = 2
    while span < INV_BASE:
        ps = [_bdot(p, p) for p in ps]
        yield
        xs = [x + _bdot(x, p) for x, p in zip(xs, ps)]
        yield
        span *= 2
    blk = INV_BASE
    while blk < c:
        pair = ((row // (2 * blk)) == (col // (2 * blk))) & jnp.logical_not(
            (row // blk) == (col // blk))
        ts = [_bdot(jnp.where(pair, a, 0.0), x) for a, x in zip(mats, xs)]
        yield
        xs = [x + _bdot(x, t) for x, t in zip(xs, ts)]
        yield
        blk *= 2
    return xs


def _interleave(*gens):
    results = [None] * len(gens)
    live = list(range(len(gens)))
    while live:
        for i in list(live):
            try:
                next(gens[i])
            except StopIteration as stop:
                results[i] = stop.value
                live.remove(i)
    return results


def _wkv_kernel(rf_ref, kf_ref, vf_ref, sf_ref, rb_ref, kb_ref, vb_ref, sb_ref,
                w0_ref, w2_ref, a0_ref, a2_ref, kk_ref, ka_ref,
                yf_ref, yb_ref, state_ref, *, chunk, heads, sub):
    c = chunk
    n = RWKV_HEAD
    width = heads * n
    rows_blk = sub * c

    @pl.when(pl.program_id(2) == 0)
    def _():
        state_ref[...] = jnp.zeros_like(state_ref)

    pc = 2 * c
    row = lax.broadcasted_iota(jnp.int32, (pc, pc), 0)
    col = lax.broadcasted_iota(jnp.int32, (pc, pc), 1)
    same_head = (row // c) == (col // c)
    lane = lax.broadcasted_iota(jnp.int32, (c, LANES), 1)
    head_a = lane < n

    def stack(x):
        zero = jnp.zeros_like(x)
        return jnp.concatenate([jnp.where(head_a, x, zero), jnp.where(head_a, zero, x)], axis=0)

    lrow = lax.broadcasted_iota(jnp.int32, (width, width), 0)
    lcol = lax.broadcasted_iota(jnp.int32, (width, width), 1)
    seg_ones = ((lrow // n) == (lcol // n)).astype(BF16)
    brow = lax.broadcasted_iota(jnp.int32, (rows_blk, rows_blk), 0)
    bcol = lax.broadcasted_iota(jnp.int32, (rows_blk, rows_blk), 1)
    same_chunk = (brow // c) == (bcol // c)
    ones_c = jnp.ones((c, LANES), BF16)
    tn = (((0,), (0,)), ((), ()))

    def chunk_local(us):
        gs = [_bdot_nt(u["lhs"], u["rhs"]) for u in us]
        yield
        a_ab = [jnp.where(u["strict"], g[:pc, :pc], 0.0) for u, g in zip(us, gs)]
        a_ak = [jnp.where(u["strict"], g[:pc, pc:], 0.0).astype(BF16) for u, g in zip(us, gs)]
        a_r = [jnp.concatenate([jnp.where(u["incl"], g[pc:, :pc], 0.0),
                                jnp.where(u["incl"], g[pc:, pc:], 0.0)], axis=1).astype(BF16)
               for u, g in zip(us, gs)]
        pv = [_bdot(a, u["v"]) for a, u in zip(a_ak, us)]
        yield
        tinv = yield from _unit_tri_inverses(a_ab, row, col, c)
        qs = [_bdot(t, jnp.concatenate([u["at"], p.astype(BF16)], axis=1))
              for t, u, p in zip(tinv, us, pv)]
        yield
        zs = [jnp.concatenate([q.astype(BF16),
                               jnp.concatenate([jnp.zeros((pc, LANES), BF16), u["v"]], axis=1)],
                              axis=0)
              for q, u in zip(qs, us)]
        ry = [_bdot(a, z) for a, z in zip(a_r, zs)]
        yield
        ms = [_bdot_tn(u["end"], z) for u, z in zip(us, zs)]
        return ry, ms

    refs = ((rf_ref, kf_ref, vf_ref, sf_ref), (rb_ref, kb_ref, vb_ref, sb_ref))

    def prepare(d):
        r_ref, k_ref, v_ref, s_ref = refs[d]
        units = []
        strict = same_head & ((row > col) if d == 0 else (row < col))
        incl = same_head & ((row >= col) if d == 0 else (row <= col))
        r = r_ref[...].astype(F32)
        k = k_ref[...].astype(F32)
        vb = v_ref[...]
        sm = s_ref[...]
        wl = sm[:, d * LORA:(d + 1) * LORA]
        al = sm[:, (2 + d) * LORA:(3 + d) * LORA]
        w_pre = w0_ref[d:d + 1, :] + _bdot(jnp.tanh(wl), w2_ref[d])
        logw = -EXP_NEG_HALF * _sigmoid(w_pre)
        yield
        a = _sigmoid(a0_ref[d:d + 1, :] + _bdot(al, a2_ref[d]))
        kk = k * kk_ref[...]
        ss = jnp.dot((kk * kk).astype(BF16), seg_ones, preferred_element_type=F32)
        yield
        kk = kk / jnp.maximum(jnp.sqrt(ss), 1e-12)
        kdir = k * (1.0 + (a - 1.0) * ka_ref[...])
        b = kk * a
        yield
        tri = (same_chunk & ((brow >= bcol) if d == 0 else (brow <= bcol))).astype(BF16)
        l_hi, l_lo = _split2(logw)
        cl = (jnp.dot(tri, l_hi, preferred_element_type=F32)
              + jnp.dot(tri, l_lo, preferred_element_type=F32))
        ends = []
        for ci in range(sub):
            last = ci * c + (c - 1 if d == 0 else 0)
            ends.append(jnp.broadcast_to(cl[last:last + 1, :], (c, width)))
        cl_end = jnp.concatenate(ends, axis=0)
        yield
        rt = (r * jnp.exp(cl)).astype(BF16)
        yield
        at = (-kk * jnp.exp(cl - logw)).astype(BF16)
        yield
        e_neg = jnp.exp(-cl)
        bt = (b * e_neg).astype(BF16)
        kt = (kdir * e_neg).astype(BF16)
        yield
        e_end = jnp.exp(cl_end - cl)
        bh = (b * e_end).astype(BF16)
        kh = (kdir * e_end).astype(BF16)
        yield
        for ci in range(sub):
            rs = slice(ci * c, (ci + 1) * c)
            w_end_t = jnp.exp(lax.dot_general(l_hi[rs], ones_c, tn, preferred_element_type=F32)
                              + lax.dot_general(l_lo[rs], ones_c, tn, preferred_element_type=F32))
            for p in range(heads // 2):
                ls = slice(p * LANES, (p + 1) * LANES)
                at_s, rt_s = stack(at[rs, ls]), stack(rt[rs, ls])
                units.append(dict(
                    d=d, ci=ci, p=p, strict=strict, incl=incl, at=at_s, rt=rt_s,
                    v=stack(vb[rs, ls]),
                    lhs=jnp.concatenate([at_s, rt_s], axis=0),
                    rhs=jnp.concatenate([stack(bt[rs, ls]), stack(kt[rs, ls])], axis=0),
                    end=jnp.concatenate([stack(bh[rs, ls]), stack(kh[rs, ls])], axis=0),
                    w_end=w_end_t[ls, :]))
            yield
        return units

    pairs = heads // 2

    def recur(d, units, local):
        rys, mss = local
        y_ref = (yf_ref, yb_ref)[d]
        by_key = {(u["ci"], u["p"]): i for i, u in enumerate(units)}
        state = [state_ref[d, p] for p in range(pairs)]
        for ci in (range(sub) if d == 0 else range(sub - 1, -1, -1)):
            idx = [by_key[(ci, p)] for p in range(pairs)]
            ys = [_bdot(units[i]["rt"].astype(F32) + rys[i][:, :LANES], s) + rys[i][:, LANES:]
                  for i, s in zip(idx, state)]
            yield
            state = [_bdot(mss[i][:, :LANES], s) + units[i]["w_end"] * s + mss[i][:, LANES:]
                     for i, s in zip(idx, state)]
            y_ref[ci * c:(ci + 1) * c, :] = jnp.concatenate(
                [y[:c] + y[c:] for y in ys], axis=1).astype(y_ref.dtype)
            yield
        for p in range(pairs):
            state_ref[d, p] = state[p]

    units_f, = _interleave(prepare(0))
    local_f, units_b = _interleave(chunk_local(units_f), prepare(1))
    local_b, _ = _interleave(chunk_local(units_b), recur(0, units_f, local_f))
    _interleave(recur(1, units_b, local_b))


def _wkv(proj, small, w0, w2, a0, a2, k_k, k_a, batch, seq):
    n_tok = proj.shape[0]
    hg = WKV_HEADS_PER_STEP
    width = hg * RWKV_HEAD
    rows = WKV_CHUNK * WKV_CHUNKS_PER_STEP
    nb = seq // rows
    groups = D_INNER // width

    def fwd(s):
        return lambda b, g, t: (b * nb + t, s * groups + g)

    def bwd(s):
        return lambda b, g, t: (b * nb + (nb - 1 - t), s * groups + g)

    blk = lambda f: pl.BlockSpec((rows, width), f)
    par = lambda shape: pl.BlockSpec(shape, lambda b, g, t: (0,) * (len(shape) - 1) + (g,))
    return pl.pallas_call(
        functools.partial(_wkv_kernel, chunk=WKV_CHUNK, heads=hg, sub=WKV_CHUNKS_PER_STEP),
        grid=(batch, groups, nb),
        in_specs=[
            blk(fwd(0)), blk(fwd(1)), blk(fwd(2)),
            pl.BlockSpec((rows, 4 * LORA), lambda b, g, t: (b * nb + t, 0)),
            blk(bwd(0)), blk(bwd(1)), blk(bwd(2)),
            pl.BlockSpec((rows, 4 * LORA), lambda b, g, t: (b * nb + (nb - 1 - t), 0)),
            par((2, width)), par((2, LORA, width)), par((2, width)), par((2, LORA, width)),
            par((1, width)), par((1, width)),
        ],
        out_specs=[
            pl.BlockSpec((rows, width), lambda b, g, t: (b * nb + t, g)),
            pl.BlockSpec((rows, width), lambda b, g, t: (b * nb + (nb - 1 - t), g)),
        ],
        out_shape=[jax.ShapeDtypeStruct((n_tok, D_INNER), BF16),
                   jax.ShapeDtypeStruct((n_tok, D_INNER), BF16)],
        scratch_shapes=[pltpu.VMEM((2, hg // 2, LANES, LANES), F32)],
        compiler_params=_params(("parallel", "parallel", "arbitrary")),
        name="wkv7_chunked",
    )(proj, proj, proj, small, proj, proj, proj, small, w0, w2, a0, a2, k_k, k_a)


def _rw_post_kernel(yf_ref, yb_ref, r_ref, k_ref, v_ref, g_ref, s_ref, x_ref,
                    a0_ref, a2_ref, ka_ref, rk_ref, lg_ref, lb_ref, ind_ref, indt_ref, wo_ref,
                    o_ref):
    ind = ind_ref[...]
    indt = indt_ref[...]

    y = yf_ref[...].astype(F32) + yb_ref[...].astype(F32)
    mean = _dot_x2(_dot_x2(y, ind) * (1.0 / RWKV_HEAD), indt)
    yc = y - mean
    var = _bdot(yc * yc, ind) * (1.0 / RWKV_HEAD)
    rstd = _bdot(lax.rsqrt(var + LN_X_EPS), indt)
    yn = yc * rstd * lg_ref[...] + lb_ref[...]
    sm = s_ref[...]
    a_sum = (_sigmoid(a0_ref[0:1, :] + _bdot(sm[:, 2 * LORA:3 * LORA], a2_ref[0]))
             + _sigmoid(a0_ref[1:2, :] + _bdot(sm[:, 3 * LORA:4 * LORA], a2_ref[1])))
    k_sum = k_ref[...].astype(F32) * (2.0 + (a_sum - 2.0) * ka_ref[...])
    coef = _bdot(_bdot(r_ref[...].astype(F32) * k_sum * rk_ref[...], ind), indt)
    g = g_ref[...].astype(F32)
    out = (yn + coef * v_ref[...].astype(F32)) * (g * _sigmoid(g))
    o_ref[...] = x_ref[...] + _bdot(out, wo_ref[...])


def _rw_post(yf, yb, proj, small, x, a0, a2, k_a, r_k, lnx_g, lnx_b, ind, indt, w_out, tm=256):
    n = x.shape[0]
    e = D_INNER
    row = lambda w, j: pl.BlockSpec((tm, w), lambda i: (i, j))
    full = lambda shape: pl.BlockSpec(shape, lambda i: (0,) * len(shape))
    return pl.pallas_call(
        _rw_post_kernel,
        grid=(n // tm,),
        in_specs=[
            row(e, 0), row(e, 0),
            row(e, 0), row(e, 1), row(e, 2), row(e, 3),
            row(4 * LORA, 0), row(D_MODEL, 0),
            full((2, e)), full((2, LORA, e)), full((1, e)), full((1, e)), full((1, e)),
            full((1, e)), full((e, LANES)), full((LANES, e)), full((e, D_MODEL)),
        ],
        out_specs=row(D_MODEL, 0),
        out_shape=jax.ShapeDtypeStruct((n, D_MODEL), F32),
        compiler_params=_params(("parallel",)),
        name="rw_post",
    )(yf, yb, proj, proj, proj, proj, small, x, a0, a2, k_a, r_k, lnx_g, lnx_b, ind, indt, w_out)


def _ml_up_kernel(s_ref, cs_ref, sn_ref, qn_ref, kvn_ref, wq_ref, wkv_ref,
                  q_ref, k_ref, v_ref):
    sm = s_ref[...]
    cs = cs_ref[...]
    sn = sn_ref[...]
    scale = QK_HEAD ** -0.5 * LOG2_E

    def rms(x, g):
        return x * lax.rsqrt(jnp.mean(x * x, axis=-1, keepdims=True) + NORM_EPS) * g

    def rope(x2):
        return x2 * cs + pltpu.roll(x2, QK_ROPE, axis=1) * sn

    cq = rms(sm[:, :Q_LORA], qn_ref[...])
    ckv = rms(sm[:, Q_LORA:Q_LORA + KV_LORA], kvn_ref[...])
    q = _bdot(cq, wq_ref[...])
    kv = _bdot(ckv, wkv_ref[...])
    kpe = rope(sm[:, Q_LORA + KV_LORA:]).astype(BF16)
    lane = lax.broadcasted_iota(jnp.int32, kpe.shape, 1)
    ones_col = (lane == 0).astype(BF16)
    hv = MLA_HEADS * QK_NOPE
    for h in range(MLA_HEADS):
        o = h * Q_PAD
        q_ref[:, o:o + QK_NOPE] = (q[:, o:o + QK_NOPE] * scale).astype(BF16)
        q_ref[:, o + QK_NOPE:o + Q_PAD] = (rope(q[:, o + QK_NOPE:o + Q_PAD]) * scale).astype(BF16)
        k_ref[:, o:o + QK_NOPE] = kv[:, h * QK_NOPE:(h + 1) * QK_NOPE].astype(BF16)
        k_ref[:, o + QK_NOPE:o + Q_PAD] = kpe
        v_ref[:, o:o + V_HEAD] = kv[:, hv + h * V_HEAD:hv + (h + 1) * V_HEAD].astype(BF16)
        v_ref[:, o + V_HEAD:o + Q_PAD] = ones_col


def _ml_up(small, cs, sn, qn, kvn, wq, wkv, seq, tm=256):
    n = small.shape[0]
    per_seq = seq // tm
    row = lambda w: pl.BlockSpec((tm, w), lambda i: (i, 0))
    pos = pl.BlockSpec((tm, LANES), lambda i: (i % per_seq, 0))
    full = lambda shape: pl.BlockSpec(shape, lambda i: (0,) * len(shape))
    hq = MLA_HEADS * Q_PAD
    return pl.pallas_call(
        _ml_up_kernel,
        grid=(n // tm,),
        in_specs=[row(SMALL_COLS), pos, pos, full((1, Q_LORA)), full((1, KV_LORA)),
                  full((Q_LORA, hq)), full((KV_LORA, 2 * MLA_HEADS * V_HEAD))],
        out_specs=[row(hq), row(hq), row(hq)],
        out_shape=[jax.ShapeDtypeStruct((n, hq), BF16)] * 3,
        compiler_params=_params(("parallel",)),
        name="mla_up",
    )(small, cs, sn, qn, kvn, wq, wkv)


def _attn_kernel(q_ref, k_ref, v_ref, o_ref, m_ref, acc_ref, *, tk, parts):
    tq = q_ref.shape[0]
    rows = tq // parts
    nk = k_ref.shape[0] // tk
    nt = (((1,), (1,)), ((), ()))
    sl = [slice(a * rows, (a + 1) * rows) for a in range(parts)]
    lane_tiles = tk // LANES

    def wide(x, reps):
        return jnp.concatenate([x] * reps, axis=1)

    def row_max(s):
        return jnp.broadcast_to(jnp.max(s, axis=1, keepdims=True), (s.shape[0], LANES))

    m_ref[...] = jnp.full_like(m_ref, -jnp.inf)
    acc_ref[...] = jnp.zeros_like(acc_ref)

    def body(j, carry):
        off = pl.multiple_of(j * tk, tk)
        kj = k_ref[pl.ds(off, tk), :]
        vj = v_ref[pl.ds(off, tk), :]
        ss = [lax.dot_general(q_ref[r, :], kj, nt, preferred_element_type=F32) for r in sl]
        for r, s in zip(sl, ss):
            mo = m_ref[r, :]
            mn = jnp.maximum(mo, row_max(s))
            p = jnp.exp2(s - wide(mn, lane_tiles)).astype(BF16)
            acc_ref[r, :] = (acc_ref[r, :] * wide(jnp.exp2(mo - mn), Q_PAD // LANES)
                             + jnp.dot(p, vj, preferred_element_type=F32))
            m_ref[r, :] = mn
        return carry

    lax.fori_loop(0, nk, body, 0, unroll=2 if nk % 2 == 0 else 1)
    acc = acc_ref[...]
    o_ref[...] = (acc[:, :V_HEAD] / acc[:, V_HEAD:V_HEAD + 1]).astype(o_ref.dtype)


def _attention(q, k, v, batch, seq, tq=1024, tk=2048, parts=2):
    n = q.shape[0]
    tq = min(tq, seq)
    tk = min(tk, seq)
    nq = seq // tq
    return pl.pallas_call(
        functools.partial(_attn_kernel, tk=tk, parts=parts),
        grid=(batch, MLA_HEADS, nq),
        in_specs=[
            pl.BlockSpec((tq, Q_PAD), lambda b, h, i: (b * nq + i, h)),
            pl.BlockSpec((seq, Q_PAD), lambda b, h, i: (b, h)),
            pl.BlockSpec((seq, Q_PAD), lambda b, h, i: (b, h)),
        ],
        out_specs=pl.BlockSpec((tq, V_HEAD), lambda b, h, i: (b * nq + i, h)),
        out_shape=jax.ShapeDtypeStruct((n, MLA_HEADS * V_HEAD), BF16),
        scratch_shapes=[pltpu.VMEM((tq, LANES), F32), pltpu.VMEM((tq, Q_PAD), F32)],
        compiler_params=_params(("parallel", "parallel", "parallel")),
        name="mla_flash_attention",
    )(q, k, v)


def _ml_post_kernel(o_ref, g_ref, x_ref, wo_ref, fg_ref, y_ref, *, final):
    g = g_ref[...].astype(F32)
    out = o_ref[...].astype(F32) * (g * _sigmoid(g))
    y = x_ref[...] + _bdot(out, wo_ref[...])
    if final:
        y = _rms(y, fg_ref[...])
    y_ref[...] = y


def _ml_post(o, g, x, w_out, final_g, final, tm=512):
    n = x.shape[0]
    row = lambda w: pl.BlockSpec((tm, w), lambda i: (i, 0))
    full = lambda shape: pl.BlockSpec(shape, lambda i: (0,) * len(shape))
    return pl.pallas_call(
        functools.partial(_ml_post_kernel, final=final),
        grid=(n // tm,),
        in_specs=[row(D_INNER), row(D_INNER), row(D_MODEL), full((D_INNER, D_MODEL)),
                  full((1, D_MODEL))],
        out_specs=row(D_MODEL),
        out_shape=jax.ShapeDtypeStruct((n, D_MODEL), F32),
        compiler_params=_params(("parallel",)),
        name="mla_post",
    )(o, g, x, w_out, final_g)


def _rotate_half_cols(w):
    half = QK_ROPE // 2
    return jnp.concatenate([-w[..., half:], w[..., :half]], axis=-1)


def _prep_rwkv(j, rw_in, rw_w0, rw_w2, rw_a0, rw_a2, rw_kk, rw_ka, rw_rk, rw_lnx_g, rw_lnx_b,
               rw_out):
    e = D_INNER
    w = rw_in[j]
    w_main = w[:, :4 * e].astype(BF16)
    ww = w[:, 4 * e:4 * e + 2 * LORA]
    wa = w[:, 4 * e + 2 * LORA:]
    z = jnp.zeros_like(ww)
    w_small = jnp.concatenate([jnp.concatenate([ww, z], axis=1),
                               jnp.concatenate([z, wa], axis=1)], axis=0).astype(BF16)
    return dict(
        w_main=w_main, w_small=w_small, w0=rw_w0[j], w2=rw_w2[j].astype(BF16), a0=rw_a0[j],
        a2=rw_a2[j].astype(BF16), k_k=rw_kk[j][None], k_a=rw_ka[j][None],
        r_k=rw_rk[j].reshape(1, e), lnx_g=rw_lnx_g[j][None], lnx_b=rw_lnx_b[j][None],
        w_out=rw_out[j].astype(BF16))


def _prep_mla(j, ml_in, ml_qn, ml_kvn, ml_uq, ml_ukv, ml_out):
    w = ml_in[j]
    lo = Q_LORA + KV_LORA
    w_kpe = w[:, lo:lo + QK_ROPE]
    w_small = jnp.concatenate([w[:, :lo], w_kpe, _rotate_half_cols(w_kpe)], axis=1).astype(BF16)
    w_g = w[:, lo + QK_ROPE:].astype(BF16)
    uq = ml_uq[j].reshape(Q_LORA, MLA_HEADS, QK_HEAD)
    pe = uq[..., QK_NOPE:]
    wq = jnp.concatenate([uq[..., :QK_NOPE], pe, _rotate_half_cols(pe)], axis=-1)
    wq = wq.reshape(Q_LORA, MLA_HEADS * Q_PAD).astype(BF16)
    ukv = ml_ukv[j].reshape(KV_LORA, MLA_HEADS, QK_NOPE + V_HEAD)
    wkv = jnp.concatenate([ukv[..., :QK_NOPE].reshape(KV_LORA, -1),
                           ukv[..., QK_NOPE:].reshape(KV_LORA, -1)], axis=1).astype(BF16)
    return dict(w_small=w_small, w_g=w_g, qn=ml_qn[j][None], kvn=ml_kvn[j][None], wq=wq,
                wkv=wkv, w_out=ml_out[j].astype(BF16))


def _rope_tables(seq):
    inv_freq = 1.0 / (ROPE_THETA ** (jnp.arange(0, QK_ROPE, 2, dtype=F32) / QK_ROPE))
    ang = jnp.arange(seq, dtype=F32)[:, None] * inv_freq[None, :]
    z = jnp.zeros((seq, LANES - QK_ROPE), F32)
    cs = jnp.concatenate([jnp.cos(ang), jnp.cos(ang), z], axis=1)
    sn = jnp.concatenate([jnp.sin(ang), jnp.sin(ang), z], axis=1)
    return cs, sn


def _head_indicator():
    e = lax.broadcasted_iota(jnp.int32, (D_INNER, LANES), 0)
    h = lax.broadcasted_iota(jnp.int32, (D_INNER, LANES), 1)
    ind = ((e // RWKV_HEAD) == h).astype(BF16)
    return ind, ind.T


def _trunk(x3, ln_g, final_g, rw, ml):
    batch, seq, _ = x3.shape
    x = x3.reshape(batch * seq, D_MODEL)
    ind, indt = _head_indicator()
    cs, sn = _rope_tables(seq)
    for i in range(DEPTH):
        g = ln_g[i][None]
        if i % 2 == 0:
            p = rw[i // 2]
            xs4, xs2 = _rw_prep(x, g, p["mu"], seq)
            proj = _matmul_streams(xs4, p["w_main"])
            small = _matmul(xs2, p["w_small"])
            yf, yb = _wkv(proj, small, p["w0"], p["w2"], p["a0"], p["a2"], p["k_k"], p["k_a"],
                          batch, seq)
            x = _rw_post(yf, yb, proj, small, x, p["a0"], p["a2"], p["k_a"], p["r_k"],
                         p["lnx_g"], p["lnx_b"], ind, indt, p["w_out"])
        else:
            p = ml[i // 2]
            h = _rmsnorm_bf16(x, g)
            small = _matmul(h, p["w_small"], tn=SMALL_COLS)
            gate = _matmul(h, p["w_g"], out_dtype=BF16)
            q, k, v = _ml_up(small, cs, sn, p["qn"], p["kvn"], p["wq"], p["wkv"], seq)
            o = _attention(q, k, v, batch, seq)
            x = _ml_post(o, gate, x, p["w_out"], final_g[None], final=(i == DEPTH - 1))
    return x.reshape(batch, seq, D_MODEL)


def kernel(x_prompt, x_sample, ln_g, final_g, rw_mu, rw_in, rw_w0, rw_w2, rw_a0, rw_a2, rw_kk,
           rw_ka, rw_rk, rw_lnx_g, rw_lnx_b, rw_out, ml_in, ml_qn, ml_kvn, ml_uq, ml_ukv, ml_out):
    rw = []
    for j in range(rw_in.shape[0]):
        p = _prep_rwkv(j, rw_in, rw_w0, rw_w2, rw_a0, rw_a2, rw_kk, rw_ka, rw_rk, rw_lnx_g,
                       rw_lnx_b, rw_out)
        p["mu"] = rw_mu[j]
        rw.append(p)
    ml = [_prep_mla(j, ml_in, ml_qn, ml_kvn, ml_uq, ml_ukv, ml_out) for j in range(ml_in.shape[0])]
    y_prompt = _trunk(x_prompt, ln_g, final_g, rw, ml)
    y_sample = _trunk(x_sample, ln_g, final_g, rw, ml)
    return (y_prompt, y_sample)
```

```python
import functools

import jax
import jax.numpy as jnp
from jax import lax
from jax.experimental import pallas as pl
from jax.experimental.pallas import tpu as pltpu

F32 = jnp.float32
BF16 = jnp.bfloat16

D_MODEL = 1024
D_INNER = 2048
DEPTH = 4
RWKV_HEAD = 64
RWKV_HEADS = D_INNER // RWKV_HEAD
LORA = 64
LN_X_EPS = 64e-5
NORM_EPS = 1e-6
MLA_HEADS = 16
QK_NOPE = 128
QK_ROPE = 64
V_HEAD = 128
QK_HEAD = QK_NOPE + QK_ROPE
Q_LORA = 384
KV_LORA = 256
ROPE_THETA = 10000.0
Q_PAD = 256
SMALL_COLS = Q_LORA + KV_LORA + 2 * QK_ROPE

LANES = 128
SUBLANES = 8
WKV_CHUNK = 64
WKV_HEADS_PER_STEP = 8
WKV_CHUNKS_PER_STEP = 4
INV_BASE = 16
EXP_NEG_HALF = 0.6065306597126334
LOG2_E = 1.4426950408889634
VMEM_LIMIT = 48 * 1024 * 1024

HIGHEST = lax.Precision.HIGHEST


def _bdot(a, b):
    return jnp.dot(a.astype(BF16), b.astype(BF16), preferred_element_type=F32)


def _bdot_nt(a, b):
    return lax.dot_general(a.astype(BF16), b.astype(BF16), (((1,), (1,)), ((), ())),
                           preferred_element_type=F32)


def _bdot_tn(a, b):
    return lax.dot_general(a.astype(BF16), b.astype(BF16), (((0,), (0,)), ((), ())),
                           preferred_element_type=F32)


def _split2(x):
    hi = x.astype(BF16)
    lo = (x - hi.astype(F32)).astype(BF16)
    return hi, lo


def _dot_x2(x, m_bf16):
    hi, lo = _split2(x)
    return (jnp.dot(hi, m_bf16, preferred_element_type=F32)
            + jnp.dot(lo, m_bf16, preferred_element_type=F32))


def _sigmoid(x):
    return 1.0 / (1.0 + jnp.exp(-x))


def _params(sem):
    return pltpu.CompilerParams(dimension_semantics=sem, vmem_limit_bytes=VMEM_LIMIT)


def _rms(x, g):
    return x * lax.rsqrt(jnp.mean(x * x, axis=-1, keepdims=True) + NORM_EPS) * g


def _rw_prep_kernel(x_ref, xp_ref, xn_ref, g_ref, mu_ref, xs4_ref, xs2_ref, *, tm, seq):
    i = pl.program_id(0)
    g = g_ref[...]
    h = _rms(x_ref[...], g)
    hp = _rms(xp_ref[SUBLANES - 1:SUBLANES, :], g)
    hn = _rms(xn_ref[0:1, :], g)
    row0 = i * tm
    hp = jnp.where((row0 % seq) == 0, 0.0, hp)
    hn = jnp.where(((row0 + tm) % seq) == 0, 0.0, hn)
    rid = lax.broadcasted_iota(jnp.int32, (tm, 1), 0)
    prev = jnp.where(rid == 0, hp, pltpu.roll(h, 1, axis=0))
    nxt = jnp.where(rid == tm - 1, hn, pltpu.roll(h, tm - 1, axis=0))
    xx = 0.5 * (prev + nxt) - h
    mu = mu_ref[...]
    for n, s in enumerate((0, 2, 3, 5)):
        xs4_ref[n] = (h + mu[s:s + 1, :] * xx).astype(BF16)
    xs2_ref[:, :D_MODEL] = (h + mu[1:2, :] * xx).astype(BF16)
    xs2_ref[:, D_MODEL:] = (h + mu[4:5, :] * xx).astype(BF16)


def _rw_prep(x, ln_g, mu, seq, tm=256):
    n = x.shape[0]
    nb8 = n // SUBLANES
    per = tm // SUBLANES
    return pl.pallas_call(
        functools.partial(_rw_prep_kernel, tm=tm, seq=seq),
        grid=(n // tm,),
        in_specs=[
            pl.BlockSpec((tm, D_MODEL), lambda i: (i, 0)),
            pl.BlockSpec((SUBLANES, D_MODEL), lambda i: (jnp.maximum(i * per - 1, 0), 0)),
            pl.BlockSpec((SUBLANES, D_MODEL), lambda i: (jnp.minimum((i + 1) * per, nb8 - 1), 0)),
            pl.BlockSpec((1, D_MODEL), lambda i: (0, 0)),
            pl.BlockSpec((6, D_MODEL), lambda i: (0, 0)),
        ],
        out_specs=[
            pl.BlockSpec((4, tm, D_MODEL), lambda i: (0, i, 0)),
            pl.BlockSpec((tm, 2 * D_MODEL), lambda i: (i, 0)),
        ],
        out_shape=[
            jax.ShapeDtypeStruct((4, n, D_MODEL), BF16),
            jax.ShapeDtypeStruct((n, 2 * D_MODEL), BF16),
        ],
        compiler_params=_params(("parallel",)),
        name="rw_prep",
    )(x, x, x, ln_g, mu)


def _norm_kernel(x_ref, g_ref, o_ref):
    o_ref[...] = _rms(x_ref[...], g_ref[...]).astype(o_ref.dtype)


def _rmsnorm_bf16(x, g, tm=512):
    n = x.shape[0]
    return pl.pallas_call(
        _norm_kernel,
        grid=(n // tm,),
        in_specs=[pl.BlockSpec((tm, D_MODEL), lambda i: (i, 0)),
                  pl.BlockSpec((1, D_MODEL), lambda i: (0, 0))],
        out_specs=pl.BlockSpec((tm, D_MODEL), lambda i: (i, 0)),
        out_shape=jax.ShapeDtypeStruct((n, D_MODEL), BF16),
        compiler_params=_params(("parallel",)),
        name="rmsnorm",
    )(x, g)


def _mm_kernel(x_ref, w_ref, o_ref):
    o_ref[...] = jnp.dot(x_ref[...], w_ref[...], preferred_element_type=F32).astype(o_ref.dtype)


def _matmul(x, w, out_dtype=F32, tm=1024, tn=512):
    n, k = x.shape
    m = w.shape[1]
    tn = min(tn, m)
    return pl.pallas_call(
        _mm_kernel,
        grid=(n // tm, m // tn),
        in_specs=[pl.BlockSpec((tm, k), lambda i, j: (i, 0)),
                  pl.BlockSpec((k, tn), lambda i, j: (0, j))],
        out_specs=pl.BlockSpec((tm, tn), lambda i, j: (i, j)),
        out_shape=jax.ShapeDtypeStruct((n, m), out_dtype),
        compiler_params=_params(("parallel", "parallel")),
        name="matmul",
    )(x, w)


def _matmul_streams(xs, w, tm=2048, tn=512):
    ns, n, k = xs.shape
    per = D_INNER // tn
    return pl.pallas_call(
        _mm_kernel,
        grid=(n // tm, ns, per),
        in_specs=[pl.BlockSpec((None, tm, k), lambda i, s, j: (s, i, 0)),
                  pl.BlockSpec((k, tn), lambda i, s, j: (0, s * per + j))],
        out_specs=pl.BlockSpec((tm, tn), lambda i, s, j: (i, s * per + j)),
        out_shape=jax.ShapeDtypeStruct((n, ns * D_INNER), BF16),
        compiler_params=_params(("parallel", "parallel", "parallel")),
        name="matmul_streams",
    )(xs, w)


def _unit_tri_inverses(mats, row, col, c):
    eye = (row == col).astype(F32)
    same = (row // INV_BASE) == (col // INV_BASE)
    ps = [jnp.where(same, a, 0.0) for a in mats]
    xs = [eye + p for p in ps]
    span = 2
    while span < INV_BASE:
        ps = [_bdot(p, p) for p in ps]
        yield
        xs = [x + _bdot(x, p) for x, p in zip(xs, ps)]
        yield
        span *= 2
    blk = INV_BASE
    while blk < c:
        pair = ((row // (2 * blk)) == (col // (2 * blk))) & jnp.logical_not(
            (row // blk) == (col // blk))
        ts = [_bdot(jnp.where(pair, a, 0.0), x) for a, x in zip(mats, xs)]
        yield
        xs = [x + _bdot(x, t) for x, t in zip(xs, ts)]
        yield
        blk *= 2
    return xs


def _interleave(*gens):
    results = [None] * len(gens)
    live = list(range(len(gens)))
    while live:
        for i in list(live):
            try:
                next(gens[i])
            except StopIteration as stop:
                results[i] = stop.value
                live.remove(i)
    return results


def _wkv_kernel(rf_ref, kf_ref, vf_ref, sf_ref, rb_ref, kb_ref, vb_ref, sb_ref,
                w0_ref, w2_ref, a0_ref, a2_ref, kk_ref, ka_ref,
                yf_ref, yb_ref, state_ref, *, chunk, heads, sub):
    c = chunk
    n = RWKV_HEAD
    width = heads * n
    rows_blk = sub * c

    @pl.when(pl.program_id(2) == 0)
    def _():
        state_ref[...] = jnp.zeros_like(state_ref)

    pc = 2 * c
    row = lax.broadcasted_iota(jnp.int32, (pc, pc), 0)
    col = lax.broadcasted_iota(jnp.int32, (pc, pc), 1)
    same_head = (row // c) == (col // c)
    lane = lax.broadcasted_iota(jnp.int32, (c, LANES), 1)
    head_a = lane < n

    def stack(x):
        zero = jnp.zeros_like(x)
        return jnp.concatenate([jnp.where(head_a, x, zero), jnp.where(head_a, zero, x)], axis=0)

    lrow = lax.broadcasted_iota(jnp.int32, (width, width), 0)
    lcol = lax.broadcasted_iota(jnp.int32, (width, width), 1)
    seg_ones = ((lrow // n) == (lcol // n)).astype(BF16)
    brow = lax.broadcasted_iota(jnp.int32, (rows_blk, rows_blk), 0)
    bcol = lax.broadcasted_iota(jnp.int32, (rows_blk, rows_blk), 1)
    same_chunk = (brow // c) == (bcol // c)
    ones_c = jnp.ones((c, LANES), BF16)
    tn = (((0,), (0,)), ((), ()))

    def chunk_local(us):
        gs = [_bdot_nt(u["lhs"], u["rhs"]) for u in us]
        yield
        a_ab = [jnp.where(u["strict"], g[:pc, :pc], 0.0) for u, g in zip(us, gs)]
        a_ak = [jnp.where(u["strict"], g[:pc, pc:], 0.0).astype(BF16) for u, g in zip(us, gs)]
        a_r = [jnp.concatenate([jnp.where(u["incl"], g[pc:, :pc], 0.0),
                                jnp.where(u["incl"], g[pc:, pc:], 0.0)], axis=1).astype(BF16)
               for u, g in zip(us, gs)]
        pv = [_bdot(a, u["v"]) for a, u in zip(a_ak, us)]
        yield
        tinv = yield from _unit_tri_inverses(a_ab, row, col, c)
        qs = [_bdot(t, jnp.concatenate([u["at"], p.astype(BF16)], axis=1))
              for t, u, p in zip(tinv, us, pv)]
        yield
        zs = [jnp.concatenate([q.astype(BF16),
                               jnp.concatenate([jnp.zeros((pc, LANES), BF16), u["v"]], axis=1)],
                              axis=0)
              for q, u in zip(qs, us)]
        ry = [_bdot(a, z) for a, z in zip(a_r, zs)]
        yield
        ms = [_bdot_tn(u["end"], z) for u, z in zip(us, zs)]
        return ry, ms

    refs = ((rf_ref, kf_ref, vf_ref, sf_ref), (rb_ref, kb_ref, vb_ref, sb_ref))

    def prepare(d):
        r_ref, k_ref, v_ref, s_ref = refs[d]
        units = []
        strict = same_head & ((row > col) if d == 0 else (row < col))
        incl = same_head & ((row >= col) if d == 0 else (row <= col))
        r = r_ref[...].astype(F32)
        k = k_ref[...].astype(F32)
        vb = v_ref[...]
        sm = s_ref[...]
        wl = sm[:, d * LORA:(d + 1) * LORA]
        al = sm[:, (2 + d) * LORA:(3 + d) * LORA]
        w_pre = w0_ref[d:d + 1, :] + _bdot(jnp.tanh(wl), w2_ref[d])
        logw = -EXP_NEG_HALF * _sigmoid(w_pre)
        yield
        a = _sigmoid(a0_ref[d:d + 1, :] + _bdot(al, a2_ref[d]))
        kk = k * kk_ref[...]
        ss = jnp.dot((kk * kk).astype(BF16), seg_ones, preferred_element_type=F32)
        yield
        kk = kk / jnp.maximum(jnp.sqrt(ss), 1e-12)
        kdir = k * (1.0 + (a - 1.0) * ka_ref[...])
        b = kk * a
        yield
        tri = (same_chunk & ((brow >= bcol) if d == 0 else (brow <= bcol))).astype(BF16)
        l_hi, l_lo = _split2(logw)
        cl = (jnp.dot(tri, l_hi, preferred_element_type=F32)
              + jnp.dot(tri, l_lo, preferred_element_type=F32))
        ends = []
        for ci in range(sub):
            last = ci * c + (c - 1 if d == 0 else 0)
            ends.append(jnp.broadcast_to(cl[last:last + 1, :], (c, width)))
        cl_end = jnp.concatenate(ends, axis=0)
        yield
        rt = (r * jnp.exp(cl)).astype(BF16)
        yield
        at = (-kk * jnp.exp(cl - logw)).astype(BF16)
        yield
        e_neg = jnp.exp(-cl)
        bt = (b * e_neg).astype(BF16)
        kt = (kdir * e_neg).astype(BF16)
        yield
        e_end = jnp.exp(cl_end - cl)
        bh = (b * e_end).astype(BF16)
        kh = (kdir * e_end).astype(BF16)
        yield
        for ci in range(sub):
            rs = slice(ci * c, (ci + 1) * c)
            w_end_t = jnp.exp(lax.dot_general(l_hi[rs], ones_c, tn, preferred_element_type=F32)
                              + lax.dot_general(l_lo[rs], ones_c, tn, preferred_element_type=F32))
            for p in range(heads // 2):
                ls = slice(p * LANES, (p + 1) * LANES)
                at_s, rt_s = stack(at[rs, ls]), stack(rt[rs, ls])
                units.append(dict(
                    d=d, ci=ci, p=p, strict=strict, incl=incl, at=at_s, rt=rt_s,
                    v=stack(vb[rs, ls]),
                    lhs=jnp.concatenate([at_s, rt_s], axis=0),
                    rhs=jnp.concatenate([stack(bt[rs, ls]), stack(kt[rs, ls])], axis=0),
                    end=jnp.concatenate([stack(bh[rs, ls]), stack(kh[rs, ls])], axis=0),
                    w_end=w_end_t[ls, :]))
            yield
        return units

    pairs = heads // 2

    def recur(d, units, local):
        rys, mss = local
        y_ref = (yf_ref, yb_ref)[d]
        by_key = {(u["ci"], u["p"]): i for i, u in enumerate(units)}
        state = [state_ref[d, p] for p in range(pairs)]
        for ci in (range(sub) if d == 0 else range(sub - 1, -1, -1)):
            idx = [by_key[(ci, p)] for p in range(pairs)]
            ys = [_bdot(units[i]["rt"].astype(F32) + rys[i][:, :LANES], s) + rys[i][:, LANES:]
                  for i, s in zip(idx, state)]
            yield
            state = [_bdot(mss[i][:, :LANES], s) + units[i]["w_end"] * s + mss[i][:, LANES:]
                     for i, s in zip(idx, state)]
            y_ref[ci * c:(ci + 1) * c, :] = jnp.concatenate(
                [y[:c] + y[c:] for y in ys], axis=1).astype(y_ref.dtype)
            yield
        for p in range(pairs):
            state_ref[d, p] = state[p]

    units_f, = _interleave(prepare(0))
    local_f, units_b = _interleave(chunk_local(units_f), prepare(1))
    local_b, _ = _interleave(chunk_local(units_b), recur(0, units_f, local_f))
    _interleave(recur(1, units_b, local_b))


def _wkv(proj, small, w0, w2, a0, a2, k_k, k_a, batch, seq):
    n_tok = proj.shape[0]
    hg = WKV_HEADS_PER_STEP
    width = hg * RWKV_HEAD
    rows = WKV_CHUNK * WKV_CHUNKS_PER_STEP
    nb = seq // rows
    groups = D_INNER // width

    def fwd(s):
        return lambda b, g, t: (b * nb + t, s * groups + g)

    def bwd(s):
        return lambda b, g, t: (b * nb + (nb - 1 - t), s * groups + g)

    blk = lambda f: pl.BlockSpec((rows, width), f)
    par = lambda shape: pl.BlockSpec(shape, lambda b, g, t: (0,) * (len(shape) - 1) + (g,))
    return pl.pallas_call(
        functools.partial(_wkv_kernel, chunk=WKV_CHUNK, heads=hg, sub=WKV_CHUNKS_PER_STEP),
        grid=(batch, groups, nb),
        in_specs=[
            blk(fwd(0)), blk(fwd(1)), blk(fwd(2)),
            pl.BlockSpec((rows, 4 * LORA), lambda b, g, t: (b * nb + t, 0)),
            blk(bwd(0)), blk(bwd(1)), blk(bwd(2)),
            pl.BlockSpec((rows, 4 * LORA), lambda b, g, t: (b * nb + (nb - 1 - t), 0)),
            par((2, width)), par((2, LORA, width)), par((2, width)), par((2, LORA, width)),
            par((1, width)), par((1, width)),
        ],
        out_specs=[
            pl.BlockSpec((rows, width), lambda b, g, t: (b * nb + t, g)),
            pl.BlockSpec((rows, width), lambda b, g, t: (b * nb + (nb - 1 - t), g)),
        ],
        out_shape=[jax.ShapeDtypeStruct((n_tok, D_INNER), BF16),
                   jax.ShapeDtypeStruct((n_tok, D_INNER), BF16)],
        scratch_shapes=[pltpu.VMEM((2, hg // 2, LANES, LANES), F32)],
        compiler_params=_params(("parallel", "parallel", "arbitrary")),
        name="wkv7_chunked",
    )(proj, proj, proj, small, proj, proj, proj, small, w0, w2, a0, a2, k_k, k_a)


def _rw_post_kernel(yf_ref, yb_ref, r_ref, k_ref, v_ref, g_ref, s_ref, x_ref,
                    a0_ref, a2_ref, ka_ref, rk_ref, lg_ref, lb_ref, wo_ref, o_ref):
    e = D_INNER
    li = lax.broadcasted_iota(jnp.int32, (LANES, LANES), 0)
    lj = lax.broadcasted_iota(jnp.int32, (LANES, LANES), 1)
    head_ones = ((li // RWKV_HEAD) == (lj // RWKV_HEAD)).astype(BF16)

    def head_sums(z, split):
        dot = _dot_x2 if split else (lambda a, b: jnp.dot(a.astype(BF16), b,
                                                           preferred_element_type=F32))
        return jnp.concatenate([dot(z[:, o:o + LANES], head_ones) for o in range(0, e, LANES)],
                               axis=1)

    y = yf_ref[...].astype(F32) + yb_ref[...].astype(F32)
    yc = y - head_sums(y, True) * (1.0 / RWKV_HEAD)
    var = head_sums(yc * yc, False) * (1.0 / RWKV_HEAD)
    yn = yc * lax.rsqrt(var + LN_X_EPS) * lg_ref[...] + lb_ref[...]
    sm = s_ref[...]
    a_sum = (_sigmoid(a0_ref[0:1, :] + _bdot(sm[:, 2 * LORA:3 * LORA], a2_ref[0]))
             + _sigmoid(a0_ref[1:2, :] + _bdot(sm[:, 3 * LORA:4 * LORA], a2_ref[1])))
    k_sum = k_ref[...].astype(F32) * (2.0 + (a_sum - 2.0) * ka_ref[...])
    coef = head_sums(r_ref[...].astype(F32) * k_sum * rk_ref[...], False)
    g = g_ref[...].astype(F32)
    out = (yn + coef * v_ref[...].astype(F32)) * (g * _sigmoid(g))
    o_ref[...] = x_ref[...] + _bdot(out, wo_ref[...])


def _rw_post(yf, yb, proj, small, x, a0, a2, k_a, r_k, lnx_g, lnx_b, w_out, tm=256):
    n = x.shape[0]
    e = D_INNER
    row = lambda w, j: pl.BlockSpec((tm, w), lambda i: (i, j))
    full = lambda shape: pl.BlockSpec(shape, lambda i: (0,) * len(shape))
    return pl.pallas_call(
        _rw_post_kernel,
        grid=(n // tm,),
        in_specs=[
            row(e, 0), row(e, 0),
            row(e, 0), row(e, 1), row(e, 2), row(e, 3),
            row(4 * LORA, 0), row(D_MODEL, 0),
            full((2, e)), full((2, LORA, e)), full((1, e)), full((1, e)), full((1, e)),
            full((1, e)), full((e, D_MODEL)),
        ],
        out_specs=row(D_MODEL, 0),
        out_shape=jax.ShapeDtypeStruct((n, D_MODEL), F32),
        compiler_params=_params(("parallel",)),
        name="rw_post",
    )(yf, yb, proj, proj, proj, proj, small, x, a0, a2, k_a, r_k, lnx_g, lnx_b, w_out)


def _ml_up_kernel(s_ref, cs_ref, sn_ref, qn_ref, kvn_ref, wq_ref, wkv_ref,
                  q_ref, k_ref, v_ref):
    sm = s_ref[...]
    cs = cs_ref[...]
    sn = sn_ref[...]
    scale = QK_HEAD ** -0.5 * LOG2_E

    def rms(x, g):
        return x * lax.rsqrt(jnp.mean(x * x, axis=-1, keepdims=True) + NORM_EPS) * g

    def rope(x2):
        return x2 * cs + pltpu.roll(x2, QK_ROPE, axis=1) * sn

    cq = rms(sm[:, :Q_LORA], qn_ref[...])
    ckv = rms(sm[:, Q_LORA:Q_LORA + KV_LORA], kvn_ref[...])
    q = _bdot(cq, wq_ref[...])
    kv = _bdot(ckv, wkv_ref[...])
    kpe = rope(sm[:, Q_LORA + KV_LORA:]).astype(BF16)
    lane = lax.broadcasted_iota(jnp.int32, kpe.shape, 1)
    ones_col = (lane == 0).astype(BF16)
    hv = MLA_HEADS * QK_NOPE
    for h in range(MLA_HEADS):
        o = h * Q_PAD
        q_ref[:, o:o + QK_NOPE] = (q[:, o:o + QK_NOPE] * scale).astype(BF16)
        q_ref[:, o + QK_NOPE:o + Q_PAD] = (rope(q[:, o + QK_NOPE:o + Q_PAD]) * scale).astype(BF16)
        k_ref[:, o:o + QK_NOPE] = kv[:, h * QK_NOPE:(h + 1) * QK_NOPE].astype(BF16)
        k_ref[:, o + QK_NOPE:o + Q_PAD] = kpe
        v_ref[:, o:o + V_HEAD] = kv[:, hv + h * V_HEAD:hv + (h + 1) * V_HEAD].astype(BF16)
        v_ref[:, o + V_HEAD:o + Q_PAD] = ones_col


def _ml_up(small, cs, sn, qn, kvn, wq, wkv, seq, tm=256):
    n = small.shape[0]
    per_seq = seq // tm
    row = lambda w: pl.BlockSpec((tm, w), lambda i: (i, 0))
    pos = pl.BlockSpec((tm, LANES), lambda i: (i % per_seq, 0))
    full = lambda shape: pl.BlockSpec(shape, lambda i: (0,) * len(shape))
    hq = MLA_HEADS * Q_PAD
    return pl.pallas_call(
        _ml_up_kernel,
        grid=(n // tm,),
        in_specs=[row(SMALL_COLS), pos, pos, full((1, Q_LORA)), full((1, KV_LORA)),
                  full((Q_LORA, hq)), full((KV_LORA, 2 * MLA_HEADS * V_HEAD))],
        out_specs=[row(hq), row(hq), row(hq)],
        out_shape=[jax.ShapeDtypeStruct((n, hq), BF16)] * 3,
        compiler_params=_params(("parallel",)),
        name="mla_up",
    )(small, cs, sn, qn, kvn, wq, wkv)


def _attn_kernel(q_ref, k_ref, v_ref, o_ref, m_ref, acc_ref, *, tk, parts):
    tq = q_ref.shape[0]
    rows = tq // parts
    nk = k_ref.shape[0] // tk
    nt = (((1,), (1,)), ((), ()))
    sl = [slice(a * rows, (a + 1) * rows) for a in range(parts)]
    lane_tiles = tk // LANES

    def wide(x, reps):
        return jnp.concatenate([x] * reps, axis=1)

    def row_max(s):
        return jnp.broadcast_to(jnp.max(s, axis=1, keepdims=True), (s.shape[0], LANES))

    m_ref[...] = jnp.full_like(m_ref, -jnp.inf)
    acc_ref[...] = jnp.zeros_like(acc_ref)

    def body(j, carry):
        off = pl.multiple_of(j * tk, tk)
        kj = k_ref[pl.ds(off, tk), :]
        vj = v_ref[pl.ds(off, tk), :]
        ss = [lax.dot_general(q_ref[r, :], kj, nt, preferred_element_type=F32) for r in sl]
        for r, s in zip(sl, ss):
            mo = m_ref[r, :]
            mn = jnp.maximum(mo, row_max(s))
            p = jnp.exp2(s - wide(mn, lane_tiles)).astype(BF16)
            acc_ref[r, :] = (acc_ref[r, :] * wide(jnp.exp2(mo - mn), Q_PAD // LANES)
                             + jnp.dot(p, vj, preferred_element_type=F32))
            m_ref[r, :] = mn
        return carry

    lax.fori_loop(0, nk, body, 0, unroll=True)
    acc = acc_ref[...]
    o_ref[...] = (acc[:, :V_HEAD] / acc[:, V_HEAD:V_HEAD + 1]).astype(o_ref.dtype)


def _attention(q, k, v, batch, seq, tq=1024, tk=2048, parts=2):
    n = q.shape[0]
    tq = min(tq, seq)
    tk = min(tk, seq)
    nq = seq // tq
    return pl.pallas_call(
        functools.partial(_attn_kernel, tk=tk, parts=parts),
        grid=(batch, MLA_HEADS, nq),
        in_specs=[
            pl.BlockSpec((tq, Q_PAD), lambda b, h, i: (b * nq + i, h)),
            pl.BlockSpec((seq, Q_PAD), lambda b, h, i: (b, h)),
            pl.BlockSpec((seq, Q_PAD), lambda b, h, i: (b, h)),
        ],
        out_specs=pl.BlockSpec((tq, V_HEAD), lambda b, h, i: (b * nq + i, h)),
        out_shape=jax.ShapeDtypeStruct((n, MLA_HEADS * V_HEAD), BF16),
        scratch_shapes=[pltpu.VMEM((tq, LANES), F32), pltpu.VMEM((tq, Q_PAD), F32)],
        compiler_params=_params(("parallel", "parallel", "parallel")),
        name="mla_flash_attention",
    )(q, k, v)


def _ml_post_kernel(o_ref, g_ref, x_ref, wo_ref, fg_ref, y_ref, *, final):
    g = g_ref[...].astype(F32)
    out = o_ref[...].astype(F32) * (g * _sigmoid(g))
    y = x_ref[...] + _bdot(out, wo_ref[...])
    if final:
        y = _rms(y, fg_ref[...])
    y_ref[...] = y


def _ml_post(o, g, x, w_out, final_g, final, tm=512):
    n = x.shape[0]
    row = lambda w: pl.BlockSpec((tm, w), lambda i: (i, 0))
    full = lambda shape: pl.BlockSpec(shape, lambda i: (0,) * len(shape))
    return pl.pallas_call(
        functools.partial(_ml_post_kernel, final=final),
        grid=(n // tm,),
        in_specs=[row(D_INNER), row(D_INNER), row(D_MODEL), full((D_INNER, D_MODEL)),
                  full((1, D_MODEL))],
        out_specs=row(D_MODEL),
        out_shape=jax.ShapeDtypeStruct((n, D_MODEL), F32),
        compiler_params=_params(("parallel",)),
        name="mla_post",
    )(o, g, x, w_out, final_g)


def _rotate_half_cols(w):
    half = QK_ROPE // 2
    return jnp.concatenate([-w[..., half:], w[..., :half]], axis=-1)


def _prep_rwkv(j, rw_in, rw_w0, rw_w2, rw_a0, rw_a2, rw_kk, rw_ka, rw_rk, rw_lnx_g, rw_lnx_b,
               rw_out):
    e = D_INNER
    w = rw_in[j]
    w_main = w[:, :4 * e].astype(BF16)
    ww = w[:, 4 * e:4 * e + 2 * LORA]
    wa = w[:, 4 * e + 2 * LORA:]
    z = jnp.zeros_like(ww)
    w_small = jnp.concatenate([jnp.concatenate([ww, z], axis=1),
                               jnp.concatenate([z, wa], axis=1)], axis=0).astype(BF16)
    return dict(
        w_main=w_main, w_small=w_small, w0=rw_w0[j], w2=rw_w2[j].astype(BF16), a0=rw_a0[j],
        a2=rw_a2[j].astype(BF16), k_k=rw_kk[j][None], k_a=rw_ka[j][None],
        r_k=rw_rk[j].reshape(1, e), lnx_g=rw_lnx_g[j][None], lnx_b=rw_lnx_b[j][None],
        w_out=rw_out[j].astype(BF16))


def _prep_mla(j, ml_in, ml_qn, ml_kvn, ml_uq, ml_ukv, ml_out):
    w = ml_in[j]
    lo = Q_LORA + KV_LORA
    w_kpe = w[:, lo:lo + QK_ROPE]
    w_small = jnp.concatenate([w[:, :lo], w_kpe, _rotate_half_cols(w_kpe)], axis=1).astype(BF16)
    w_g = w[:, lo + QK_ROPE:].astype(BF16)
    uq = ml_uq[j].reshape(Q_LORA, MLA_HEADS, QK_HEAD)
    pe = uq[..., QK_NOPE:]
    wq = jnp.concatenate([uq[..., :QK_NOPE], pe, _rotate_half_cols(pe)], axis=-1)
    wq = wq.reshape(Q_LORA, MLA_HEADS * Q_PAD).astype(BF16)
    ukv = ml_ukv[j].reshape(KV_LORA, MLA_HEADS, QK_NOPE + V_HEAD)
    wkv = jnp.concatenate([ukv[..., :QK_NOPE].reshape(KV_LORA, -1),
                           ukv[..., QK_NOPE:].reshape(KV_LORA, -1)], axis=1).astype(BF16)
    return dict(w_small=w_small, w_g=w_g, qn=ml_qn[j][None], kvn=ml_kvn[j][None], wq=wq,
                wkv=wkv, w_out=ml_out[j].astype(BF16))


def _rope_tables(seq):
    inv_freq = 1.0 / (ROPE_THETA ** (jnp.arange(0, QK_ROPE, 2, dtype=F32) / QK_ROPE))
    ang = jnp.arange(seq, dtype=F32)[:, None] * inv_freq[None, :]
    z = jnp.zeros((seq, LANES - QK_ROPE), F32)
    cs = jnp.concatenate([jnp.cos(ang), jnp.cos(ang), z], axis=1)
    sn = jnp.concatenate([jnp.sin(ang), jnp.sin(ang), z], axis=1)
    return cs, sn


def _trunk(x3, ln_g, final_g, rw, ml):
    batch, seq, _ = x3.shape
    x = x3.reshape(batch * seq, D_MODEL)
    cs, sn = _rope_tables(seq)
    for i in range(DEPTH):
        g = ln_g[i][None]
        if i % 2 == 0:
            p = rw[i // 2]
            xs4, xs2 = _rw_prep(x, g, p["mu"], seq)
            proj = _matmul_streams(xs4, p["w_main"])
            small = _matmul(xs2, p["w_small"])
            yf, yb = _wkv(proj, small, p["w0"], p["w2"], p["a0"], p["a2"], p["k_k"], p["k_a"],
                          batch, seq)
            x = _rw_post(yf, yb, proj, small, x, p["a0"], p["a2"], p["k_a"], p["r_k"],
                         p["lnx_g"], p["lnx_b"], p["w_out"])
        else:
            p = ml[i // 2]
            h = _rmsnorm_bf16(x, g)
            small = _matmul(h, p["w_small"], tn=SMALL_COLS)
            gate = _matmul(h, p["w_g"], out_dtype=BF16)
            q, k, v = _ml_up(small, cs, sn, p["qn"], p["kvn"], p["wq"], p["wkv"], seq)
            o = _attention(q, k, v, batch, seq)
            x = _ml_post(o, gate, x, p["w_out"], final_g[None], final=(i == DEPTH - 1))
    return x.reshape(batch, seq, D_MODEL)


def kernel(x_prompt, x_sample, ln_g, final_g, rw_mu, rw_in, rw_w0, rw_w2, rw_a0, rw_a2, rw_kk,
           rw_ka, rw_rk, rw_lnx_g, rw_lnx_b, rw_out, ml_in, ml_qn, ml_kvn, ml_uq, ml_ukv, ml_out):
    rw = []
    for j in range(rw_in.shape[0]):
        p = _prep_rwkv(j, rw_in, rw_w0, rw_w2, rw_a0, rw_a2, rw_kk, rw_ka, rw_rk, rw_lnx_g,
                       rw_lnx_b, rw_out)
        p["mu"] = rw_mu[j]
        rw.append(p)
    ml = [_prep_mla(j, ml_in, ml_qn, ml_kvn, ml_uq, ml_ukv, ml_out) for j in range(ml_in.shape[0])]
    y_prompt = _trunk(x_prompt, ln_g, final_g, rw, ml)
    y_sample = _trunk(x_sample, ln_g, final_g, rw, ml)
    return (y_prompt, y_sample)
```

```python
import functools

import jax
import jax.numpy as jnp
from jax import lax
from jax.experimental import pallas as pl
from jax.experimental.pallas import tpu as pltpu

F32 = jnp.float32
BF16 = jnp.bfloat16

D_MODEL = 1024
D_INNER = 2048
DEPTH = 4
RWKV_HEAD = 64
RWKV_HEADS = D_INNER // RWKV_HEAD
LORA = 64
LN_X_EPS = 64e-5
NORM_EPS = 1e-6
MLA_HEADS = 16
QK_NOPE = 128
QK_ROPE = 64
V_HEAD = 128
QK_HEAD = QK_NOPE + QK_ROPE
Q_LORA = 384
KV_LORA = 256
ROPE_THETA = 10000.0
Q_PAD = 256
SMALL_COLS = Q_LORA + KV_LORA + 2 * QK_ROPE

LANES = 128
SUBLANES = 8
WKV_CHUNK = 64
WKV_HEADS_PER_STEP = 8
WKV_CHUNKS_PER_STEP = 4
INV_BASE = 16
EXP_NEG_HALF = 0.6065306597126334
LOG2_E = 1.4426950408889634
VMEM_LIMIT = 48 * 1024 * 1024

HIGHEST = lax.Precision.HIGHEST


def _bdot(a, b):
    return jnp.dot(a.astype(BF16), b.astype(BF16), preferred_element_type=F32)


def _bdot_nt(a, b):
    return lax.dot_general(a.astype(BF16), b.astype(BF16), (((1,), (1,)), ((), ())),
                           preferred_element_type=F32)


def _bdot_tn(a, b):
    return lax.dot_general(a.astype(BF16), b.astype(BF16), (((0,), (0,)), ((), ())),
                           preferred_element_type=F32)


def _split2(x):
    hi = x.astype(BF16)
    lo = (x - hi.astype(F32)).astype(BF16)
    return hi, lo


def _dot_x2(x, m_bf16):
    hi, lo = _split2(x)
    return (jnp.dot(hi, m_bf16, preferred_element_type=F32)
            + jnp.dot(lo, m_bf16, preferred_element_type=F32))


def _sigmoid(x):
    return 1.0 / (1.0 + jnp.exp(-x))


def _params(sem):
    return pltpu.CompilerParams(dimension_semantics=sem, vmem_limit_bytes=VMEM_LIMIT)


def _rms(x, g):
    return x * lax.rsqrt(jnp.mean(x * x, axis=-1, keepdims=True) + NORM_EPS) * g


def _rw_prep_kernel(x_ref, xp_ref, xn_ref, g_ref, mu_ref, xs4_ref, xs2_ref, *, tm, seq):
    i = pl.program_id(0)
    g = g_ref[...]
    h = _rms(x_ref[...], g)
    hp = _rms(xp_ref[SUBLANES - 1:SUBLANES, :], g)
    hn = _rms(xn_ref[0:1, :], g)
    row0 = i * tm
    hp = jnp.where((row0 % seq) == 0, 0.0, hp)
    hn = jnp.where(((row0 + tm) % seq) == 0, 0.0, hn)
    rid = lax.broadcasted_iota(jnp.int32, (tm, 1), 0)
    prev = jnp.where(rid == 0, hp, pltpu.roll(h, 1, axis=0))
    nxt = jnp.where(rid == tm - 1, hn, pltpu.roll(h, tm - 1, axis=0))
    xx = 0.5 * (prev + nxt) - h
    mu = mu_ref[...]
    for n, s in enumerate((0, 2, 3, 5)):
        xs4_ref[n] = (h + mu[s:s + 1, :] * xx).astype(BF16)
    xs2_ref[:, :D_MODEL] = (h + mu[1:2, :] * xx).astype(BF16)
    xs2_ref[:, D_MODEL:] = (h + mu[4:5, :] * xx).astype(BF16)


def _rw_prep(x, ln_g, mu, seq, tm=256):
    n = x.shape[0]
    nb8 = n // SUBLANES
    per = tm // SUBLANES
    return pl.pallas_call(
        functools.partial(_rw_prep_kernel, tm=tm, seq=seq),
        grid=(n // tm,),
        in_specs=[
            pl.BlockSpec((tm, D_MODEL), lambda i: (i, 0)),
            pl.BlockSpec((SUBLANES, D_MODEL), lambda i: (jnp.maximum(i * per - 1, 0), 0)),
            pl.BlockSpec((SUBLANES, D_MODEL), lambda i: (jnp.minimum((i + 1) * per, nb8 - 1), 0)),
            pl.BlockSpec((1, D_MODEL), lambda i: (0, 0)),
            pl.BlockSpec((6, D_MODEL), lambda i: (0, 0)),
        ],
        out_specs=[
            pl.BlockSpec((4, tm, D_MODEL), lambda i: (0, i, 0)),
            pl.BlockSpec((tm, 2 * D_MODEL), lambda i: (i, 0)),
        ],
        out_shape=[
            jax.ShapeDtypeStruct((4, n, D_MODEL), BF16),
            jax.ShapeDtypeStruct((n, 2 * D_MODEL), BF16),
        ],
        compiler_params=_params(("parallel",)),
        name="rw_prep",
    )(x, x, x, ln_g, mu)


def _norm_kernel(x_ref, g_ref, o_ref):
    o_ref[...] = _rms(x_ref[...], g_ref[...]).astype(o_ref.dtype)


def _rmsnorm_bf16(x, g, tm=512):
    n = x.shape[0]
    return pl.pallas_call(
        _norm_kernel,
        grid=(n // tm,),
        in_specs=[pl.BlockSpec((tm, D_MODEL), lambda i: (i, 0)),
                  pl.BlockSpec((1, D_MODEL), lambda i: (0, 0))],
        out_specs=pl.BlockSpec((tm, D_MODEL), lambda i: (i, 0)),
        out_shape=jax.ShapeDtypeStruct((n, D_MODEL), BF16),
        compiler_params=_params(("parallel",)),
        name="rmsnorm",
    )(x, g)


def _mm_kernel(x_ref, w_ref, o_ref):
    o_ref[...] = jnp.dot(x_ref[...], w_ref[...], preferred_element_type=F32).astype(o_ref.dtype)


def _matmul(x, w, out_dtype=F32, tm=1024, tn=512):
    n, k = x.shape
    m = w.shape[1]
    tn = min(tn, m)
    return pl.pallas_call(
        _mm_kernel,
        grid=(n // tm, m // tn),
        in_specs=[pl.BlockSpec((tm, k), lambda i, j: (i, 0)),
                  pl.BlockSpec((k, tn), lambda i, j: (0, j))],
        out_specs=pl.BlockSpec((tm, tn), lambda i, j: (i, j)),
        out_shape=jax.ShapeDtypeStruct((n, m), out_dtype),
        compiler_params=_params(("parallel", "parallel")),
        name="matmul",
    )(x, w)


def _matmul_streams(xs, w, tm=2048, tn=512):
    ns, n, k = xs.shape
    per = D_INNER // tn
    return pl.pallas_call(
        _mm_kernel,
        grid=(n // tm, ns, per),
        in_specs=[pl.BlockSpec((None, tm, k), lambda i, s, j: (s, i, 0)),
                  pl.BlockSpec((k, tn), lambda i, s, j: (0, s * per + j))],
        out_specs=pl.BlockSpec((tm, tn), lambda i, s, j: (i, s * per + j)),
        out_shape=jax.ShapeDtypeStruct((n, ns * D_INNER), BF16),
        compiler_params=_params(("parallel", "parallel", "parallel")),
        name="matmul_streams",
    )(xs, w)


def _unit_tri_inverses(mats, row, col, c):
    eye = (row == col).astype(F32)
    same = (row // INV_BASE) == (col // INV_BASE)
    ps = [jnp.where(same, a, 0.0) for a in mats]
    xs = [eye + p for p in ps]
    span = 2
    while span < INV_BASE:
        ps = [_bdot(p, p) for p in ps]
        yield
        xs = [x + _bdot(x, p) for x, p in zip(xs, ps)]
        yield
        span *= 2
    blk = INV_BASE
    while blk < c:
        pair = ((row // (2 * blk)) == (col // (2 * blk))) & jnp.logical_not(
            (row // blk) == (col // blk))
        ts = [_bdot(jnp.where(pair, a, 0.0), x) for a, x in zip(mats, xs)]
        yield
        xs = [x + _bdot(x, t) for x, t in zip(xs, ts)]
        yield
        blk *= 2
    return xs


def _interleave(*gens):
    results = [None] * len(gens)
    live = list(range(len(gens)))
    while live:
        for i in list(live):
            try:
                next(gens[i])
            except StopIteration as stop:
                results[i] = stop.value
                live.remove(i)
    return results


def _wkv_kernel(rf_ref, kf_ref, vf_ref, sf_ref, rb_ref, kb_ref, vb_ref, sb_ref,
                w0_ref, w2_ref, a0_ref, a2_ref, kk_ref, ka_ref,
                yf_ref, yb_ref, state_ref, *, chunk, heads, sub):
    c = chunk
    n = RWKV_HEAD
    width = heads * n
    rows_blk = sub * c

    @pl.when(pl.program_id(2) == 0)
    def _():
        state_ref[...] = jnp.zeros_like(state_ref)

    pc = 2 * c
    row = lax.broadcasted_iota(jnp.int32, (pc, pc), 0)
    col = lax.broadcasted_iota(jnp.int32, (pc, pc), 1)
    same_head = (row // c) == (col // c)
    lane = lax.broadcasted_iota(jnp.int32, (c, LANES), 1)
    head_a = lane < n

    def stack(x, swap=False):
        zero = jnp.zeros_like(x)
        parts = [jnp.where(head_a, x, zero), jnp.where(head_a, zero, x)]
        return jnp.concatenate(parts[::-1] if swap else parts, axis=0)

    lrow = lax.broadcasted_iota(jnp.int32, (width, width), 0)
    lcol = lax.broadcasted_iota(jnp.int32, (width, width), 1)
    seg_ones = ((lrow // n) == (lcol // n)).astype(BF16)
    brow = lax.broadcasted_iota(jnp.int32, (rows_blk, rows_blk), 0)
    bcol = lax.broadcasted_iota(jnp.int32, (rows_blk, rows_blk), 1)
    same_chunk = (brow // c) == (bcol // c)
    chunk_sel = ((lax.broadcasted_iota(jnp.int32, (rows_blk, LANES), 0) // c)
                 == (lax.broadcasted_iota(jnp.int32, (rows_blk, LANES), 1) // (LANES // sub))
                 ).astype(BF16)
    tn = (((0,), (0,)), ((), ()))

    def chunk_local(us):
        gs = [_bdot_nt(u["lhs"], u["rhs"]) for u in us]
        yield
        x_a = [jnp.concatenate([g[:c, :pc], g[:c, pc:]], axis=0) for g in gs]
        x_r = [jnp.concatenate([g[c:, :pc], g[c:, pc:]], axis=0) for g in gs]
        a_ab = [jnp.where(u["strict"], x, 0.0) for u, x in zip(us, x_a)]
        a_ak = [jnp.where(u["strict_x"], x, 0.0).astype(BF16) for u, x in zip(us, x_a)]
        a_r = [jnp.concatenate([jnp.where(u["incl"], x, 0.0),
                                jnp.where(u["incl_x"], x, 0.0)], axis=1).astype(BF16)
               for u, x in zip(us, x_r)]
        pv = [_bdot(a, u["v"]) for a, u in zip(a_ak, us)]
        yield
        tinv = yield from _unit_tri_inverses(a_ab, row, col, c)
        qs = [_bdot(t, jnp.concatenate([u["at"], p.astype(BF16)], axis=1))
              for t, u, p in zip(tinv, us, pv)]
        yield
        zs = [jnp.concatenate([q.astype(BF16),
                               jnp.concatenate([jnp.zeros((pc, LANES), BF16), u["v"]], axis=1)],
                              axis=0)
              for q, u in zip(qs, us)]
        ry = [_bdot(a, z) for a, z in zip(a_r, zs)]
        yield
        ms = [_bdot_tn(u["end"], z) for u, z in zip(us, zs)]
        return ry, ms

    refs = ((rf_ref, kf_ref, vf_ref, sf_ref), (rb_ref, kb_ref, vb_ref, sb_ref))

    def prepare(d):
        r_ref, k_ref, v_ref, s_ref = refs[d]
        units = []
        tri_s = (row % c > col % c) if d == 0 else (row % c < col % c)
        tri_i = (row % c >= col % c) if d == 0 else (row % c <= col % c)
        other_head = jnp.logical_not(same_head)
        strict, strict_x = same_head & tri_s, other_head & tri_s
        incl, incl_x = same_head & tri_i, other_head & tri_i
        r = r_ref[...].astype(F32)
        k = k_ref[...].astype(F32)
        vb = v_ref[...]
        sm = s_ref[...]
        wl = sm[:, d * LORA:(d + 1) * LORA]
        al = sm[:, (2 + d) * LORA:(3 + d) * LORA]
        w_pre = w0_ref[d:d + 1, :] + _bdot(jnp.tanh(wl), w2_ref[d])
        logw = -EXP_NEG_HALF * _sigmoid(w_pre)
        yield
        a = _sigmoid(a0_ref[d:d + 1, :] + _bdot(al, a2_ref[d]))
        kk = k * kk_ref[...]
        ss = jnp.dot((kk * kk).astype(BF16), seg_ones, preferred_element_type=F32)
        yield
        kk = kk / jnp.maximum(jnp.sqrt(ss), 1e-12)
        kdir = k * (1.0 + (a - 1.0) * ka_ref[...])
        b = kk * a
        yield
        tri = (same_chunk & ((brow >= bcol) if d == 0 else (brow <= bcol))).astype(BF16)
        l_hi, l_lo = _split2(logw)
        cl = (jnp.dot(tri, l_hi, preferred_element_type=F32)
              + jnp.dot(tri, l_lo, preferred_element_type=F32))
        ends = []
        for ci in range(sub):
            last = ci * c + (c - 1 if d == 0 else 0)
            ends.append(jnp.broadcast_to(cl[last:last + 1, :], (c, width)))
        cl_end = jnp.concatenate(ends, axis=0)
        yield
        rt = (r * jnp.exp(cl)).astype(BF16)
        yield
        at = (-kk * jnp.exp(cl - logw)).astype(BF16)
        yield
        e_neg = jnp.exp(-cl)
        bt = (b * e_neg).astype(BF16)
        kt = (kdir * e_neg).astype(BF16)
        yield
        e_end = jnp.exp(cl_end - cl)
        bh = (b * e_end).astype(BF16)
        kh = (kdir * e_end).astype(BF16)
        yield
        w_end_t = jnp.exp(lax.dot_general(l_hi, chunk_sel, tn, preferred_element_type=F32)
                          + lax.dot_general(l_lo, chunk_sel, tn, preferred_element_type=F32))
        for ci in range(sub):
            rs = slice(ci * c, (ci + 1) * c)
            lane0 = ci * (LANES // sub)
            for p in range(heads // 2):
                ls = slice(p * LANES, (p + 1) * LANES)
                units.append(dict(
                    d=d, ci=ci, p=p, strict=strict, strict_x=strict_x, incl=incl, incl_x=incl_x,
                    at=stack(at[rs, ls]), rt=stack(rt[rs, ls]), v=stack(vb[rs, ls], swap=True),
                    lhs=jnp.concatenate([at[rs, ls], rt[rs, ls]], axis=0),
                    rhs=jnp.concatenate([stack(bt[rs, ls])[:c], stack(kt[rs, ls]),
                                         stack(bt[rs, ls])[c:]], axis=0),
                    end=jnp.concatenate([stack(bh[rs, ls]), stack(kh[rs, ls], swap=True)],
                                        axis=0),
                    w_end=jnp.broadcast_to(w_end_t[ls, lane0:lane0 + 1], (LANES, LANES))))
            yield
        return units

    pairs = heads // 2

    def recur(d, units, local):
        rys, mss = local
        y_ref = (yf_ref, yb_ref)[d]
        by_key = {(u["ci"], u["p"]): i for i, u in enumerate(units)}
        state = [state_ref[d, p] for p in range(pairs)]
        for ci in (range(sub) if d == 0 else range(sub - 1, -1, -1)):
            idx = [by_key[(ci, p)] for p in range(pairs)]
            ys = [_bdot(units[i]["rt"].astype(F32) + rys[i][:, :LANES], s) + rys[i][:, LANES:]
                  for i, s in zip(idx, state)]
            yield
            state = [_bdot(mss[i][:, :LANES], s) + units[i]["w_end"] * s + mss[i][:, LANES:]
                     for i, s in zip(idx, state)]
            y_ref[ci * c:(ci + 1) * c, :] = jnp.concatenate(
                [y[:c] + y[c:] for y in ys], axis=1).astype(y_ref.dtype)
            yield
        for p in range(pairs):
            state_ref[d, p] = state[p]

    units_f, = _interleave(prepare(0))
    local_f, units_b = _interleave(chunk_local(units_f), prepare(1))
    local_b, _ = _interleave(chunk_local(units_b), recur(0, units_f, local_f))
    _interleave(recur(1, units_b, local_b))


def _wkv(proj, small, w0, w2, a0, a2, k_k, k_a, batch, seq):
    n_tok = proj.shape[0]
    hg = WKV_HEADS_PER_STEP
    width = hg * RWKV_HEAD
    rows = WKV_CHUNK * WKV_CHUNKS_PER_STEP
    nb = seq // rows
    groups = D_INNER // width

    def fwd(s):
        return lambda b, g, t: (b * nb + t, s * groups + g)

    def bwd(s):
        return lambda b, g, t: (b * nb + (nb - 1 - t), s * groups + g)

    blk = lambda f: pl.BlockSpec((rows, width), f)
    par = lambda shape: pl.BlockSpec(shape, lambda b, g, t: (0,) * (len(shape) - 1) + (g,))
    return pl.pallas_call(
        functools.partial(_wkv_kernel, chunk=WKV_CHUNK, heads=hg, sub=WKV_CHUNKS_PER_STEP),
        grid=(batch, groups, nb),
        in_specs=[
            blk(fwd(0)), blk(fwd(1)), blk(fwd(2)),
            pl.BlockSpec((rows, 4 * LORA), lambda b, g, t: (b * nb + t, 0)),
            blk(bwd(0)), blk(bwd(1)), blk(bwd(2)),
            pl.BlockSpec((rows, 4 * LORA), lambda b, g, t: (b * nb + (nb - 1 - t), 0)),
            par((2, width)), par((2, LORA, width)), par((2, width)), par((2, LORA, width)),
            par((1, width)), par((1, width)),
        ],
        out_specs=[
            pl.BlockSpec((rows, width), lambda b, g, t: (b * nb + t, g)),
            pl.BlockSpec((rows, width), lambda b, g, t: (b * nb + (nb - 1 - t), g)),
        ],
        out_shape=[jax.ShapeDtypeStruct((n_tok, D_INNER), BF16),
                   jax.ShapeDtypeStruct((n_tok, D_INNER), BF16)],
        scratch_shapes=[pltpu.VMEM((2, hg // 2, LANES, LANES), F32)],
        compiler_params=_params(("parallel", "parallel", "arbitrary")),
        name="wkv7_chunked",
    )(proj, proj, proj, small, proj, proj, proj, small, w0, w2, a0, a2, k_k, k_a)


def _rw_post_kernel(yf_ref, yb_ref, r_ref, k_ref, v_ref, g_ref, s_ref, x_ref,
                    a0_ref, a2_ref, ka_ref, rk_ref, lg_ref, lb_ref, wo_ref, o_ref):
    e = D_INNER
    li = lax.broadcasted_iota(jnp.int32, (LANES, LANES), 0)
    lj = lax.broadcasted_iota(jnp.int32, (LANES, LANES), 1)
    head_ones = ((li // RWKV_HEAD) == (lj // RWKV_HEAD)).astype(BF16)

    def head_sums(z, split):
        dot = _dot_x2 if split else (lambda a, b: jnp.dot(a.astype(BF16), b,
                                                           preferred_element_type=F32))
        return jnp.concatenate([dot(z[:, o:o + LANES], head_ones) for o in range(0, e, LANES)],
                               axis=1)

    y = yf_ref[...].astype(F32) + yb_ref[...].astype(F32)
    yc = y - head_sums(y, True) * (1.0 / RWKV_HEAD)
    var = head_sums(yc * yc, False) * (1.0 / RWKV_HEAD)
    yn = yc * lax.rsqrt(var + LN_X_EPS) * lg_ref[...] + lb_ref[...]
    sm = s_ref[...]
    a_sum = (_sigmoid(a0_ref[0:1, :] + _bdot(sm[:, 2 * LORA:3 * LORA], a2_ref[0]))
             + _sigmoid(a0_ref[1:2, :] + _bdot(sm[:, 3 * LORA:4 * LORA], a2_ref[1])))
    k_sum = k_ref[...].astype(F32) * (2.0 + (a_sum - 2.0) * ka_ref[...])
    coef = head_sums(r_ref[...].astype(F32) * k_sum * rk_ref[...], False)
    g = g_ref[...].astype(F32)
    out = (yn + coef * v_ref[...].astype(F32)) * (g * _sigmoid(g))
    o_ref[...] = x_ref[...] + _bdot(out, wo_ref[...])


def _rw_post(yf, yb, proj, small, x, a0, a2, k_a, r_k, lnx_g, lnx_b, w_out, tm=256):
    n = x.shape[0]
    e = D_INNER
    row = lambda w, j: pl.BlockSpec((tm, w), lambda i: (i, j))
    full = lambda shape: pl.BlockSpec(shape, lambda i: (0,) * len(shape))
    return pl.pallas_call(
        _rw_post_kernel,
        grid=(n // tm,),
        in_specs=[
            row(e, 0), row(e, 0),
            row(e, 0), row(e, 1), row(e, 2), row(e, 3),
            row(4 * LORA, 0), row(D_MODEL, 0),
            full((2, e)), full((2, LORA, e)), full((1, e)), full((1, e)), full((1, e)),
            full((1, e)), full((e, D_MODEL)),
        ],
        out_specs=row(D_MODEL, 0),
        out_shape=jax.ShapeDtypeStruct((n, D_MODEL), F32),
        compiler_params=_params(("parallel",)),
        name="rw_post",
    )(yf, yb, proj, proj, proj, proj, small, x, a0, a2, k_a, r_k, lnx_g, lnx_b, w_out)


def _ml_up_kernel(s_ref, cs_ref, sn_ref, qn_ref, kvn_ref, wq_ref, wkv_ref,
                  q_ref, k_ref, v_ref):
    sm = s_ref[...]
    cs = cs_ref[...]
    sn = sn_ref[...]
    scale = QK_HEAD ** -0.5 * LOG2_E

    def rms(x, g):
        return x * lax.rsqrt(jnp.mean(x * x, axis=-1, keepdims=True) + NORM_EPS) * g

    def rope(x2):
        return x2 * cs + pltpu.roll(x2, QK_ROPE, axis=1) * sn

    cq = rms(sm[:, :Q_LORA], qn_ref[...])
    ckv = rms(sm[:, Q_LORA:Q_LORA + KV_LORA], kvn_ref[...])
    q = _bdot(cq, wq_ref[...])
    kv = _bdot(ckv, wkv_ref[...])
    kpe = rope(sm[:, Q_LORA + KV_LORA:]).astype(BF16)
    lane = lax.broadcasted_iota(jnp.int32, kpe.shape, 1)
    ones_col = (lane == 0).astype(BF16)
    hv = MLA_HEADS * QK_NOPE
    for h in range(MLA_HEADS):
        o = h * Q_PAD
        q_ref[:, o:o + QK_NOPE] = (q[:, o:o + QK_NOPE] * scale).astype(BF16)
        q_ref[:, o + QK_NOPE:o + Q_PAD] = (rope(q[:, o + QK_NOPE:o + Q_PAD]) * scale).astype(BF16)
        k_ref[:, o:o + QK_NOPE] = kv[:, h * QK_NOPE:(h + 1) * QK_NOPE].astype(BF16)
        k_ref[:, o + QK_NOPE:o + Q_PAD] = kpe
        v_ref[:, o:o + V_HEAD] = kv[:, hv + h * V_HEAD:hv + (h + 1) * V_HEAD].astype(BF16)
        v_ref[:, o + V_HEAD:o + Q_PAD] = ones_col


def _ml_up(small, cs, sn, qn, kvn, wq, wkv, seq, tm=256):
    n = small.shape[0]
    per_seq = seq // tm
    row = lambda w: pl.BlockSpec((tm, w), lambda i: (i, 0))
    pos = pl.BlockSpec((tm, LANES), lambda i: (i % per_seq, 0))
    full = lambda shape: pl.BlockSpec(shape, lambda i: (0,) * len(shape))
    hq = MLA_HEADS * Q_PAD
    return pl.pallas_call(
        _ml_up_kernel,
        grid=(n // tm,),
        in_specs=[row(SMALL_COLS), pos, pos, full((1, Q_LORA)), full((1, KV_LORA)),
                  full((Q_LORA, hq)), full((KV_LORA, 2 * MLA_HEADS * V_HEAD))],
        out_specs=[row(hq), row(hq), row(hq)],
        out_shape=[jax.ShapeDtypeStruct((n, hq), BF16)] * 3,
        compiler_params=_params(("parallel",)),
        name="mla_up",
    )(small, cs, sn, qn, kvn, wq, wkv)


def _attn_kernel(q_ref, k_ref, v_ref, o_ref, m_ref, acc_ref, *, tk, parts):
    tq = q_ref.shape[0]
    rows = tq // parts
    nk = k_ref.shape[0] // tk
    nt = (((1,), (1,)), ((), ()))
    sl = [slice(a * rows, (a + 1) * rows) for a in range(parts)]
    lane_tiles = tk // LANES

    def wide(x, reps):
        return jnp.concatenate([x] * reps, axis=1)

    def row_max(s):
        return jnp.broadcast_to(jnp.max(s, axis=1, keepdims=True), (s.shape[0], LANES))

    m_ref[...] = jnp.full_like(m_ref, -jnp.inf)
    acc_ref[...] = jnp.zeros_like(acc_ref)

    def body(j, carry):
        off = pl.multiple_of(j * tk, tk)
        kj = k_ref[pl.ds(off, tk), :]
        vj = v_ref[pl.ds(off, tk), :]
        ss = [lax.dot_general(q_ref[r, :], kj, nt, preferred_element_type=F32) for r in sl]
        for r, s in zip(sl, ss):
            mo = m_ref[r, :]
            mn = jnp.maximum(mo, row_max(s))
            p = jnp.exp2(s - wide(mn, lane_tiles)).astype(BF16)
            acc_ref[r, :] = (acc_ref[r, :] * wide(jnp.exp2(mo - mn), Q_PAD // LANES)
                             + jnp.dot(p, vj, preferred_element_type=F32))
            m_ref[r, :] = mn
        return carry

    lax.fori_loop(0, nk, body, 0, unroll=True)
    acc = acc_ref[...]
    o_ref[...] = (acc[:, :V_HEAD] / acc[:, V_HEAD:V_HEAD + 1]).astype(o_ref.dtype)


def _attention(q, k, v, batch, seq, tq=1024, tk=2048, parts=2):
    n = q.shape[0]
    tq = min(tq, seq)
    tk = min(tk, seq)
    nq = seq // tq
    return pl.pallas_call(
        functools.partial(_attn_kernel, tk=tk, parts=parts),
        grid=(batch, MLA_HEADS, nq),
        in_specs=[
            pl.BlockSpec((tq, Q_PAD), lambda b, h, i: (b * nq + i, h)),
            pl.BlockSpec((seq, Q_PAD), lambda b, h, i: (b, h)),
            pl.BlockSpec((seq, Q_PAD), lambda b, h, i: (b, h)),
        ],
        out_specs=pl.BlockSpec((tq, V_HEAD), lambda b, h, i: (b * nq + i, h)),
        out_shape=jax.ShapeDtypeStruct((n, MLA_HEADS * V_HEAD), BF16),
        scratch_shapes=[pltpu.VMEM((tq, LANES), F32), pltpu.VMEM((tq, Q_PAD), F32)],
        compiler_params=_params(("parallel", "parallel", "parallel")),
        name="mla_flash_attention",
    )(q, k, v)


def _ml_post_kernel(o_ref, g_ref, x_ref, wo_ref, fg_ref, y_ref, *, final):
    g = g_ref[...].astype(F32)
    out = o_ref[...].astype(F32) * (g * _sigmoid(g))
    y = x_ref[...] + _bdot(out, wo_ref[...])
    if final:
        y = _rms(y, fg_ref[...])
    y_ref[...] = y


def _ml_post(o, g, x, w_out, final_g, final, tm=512):
    n = x.shape[0]
    row = lambda w: pl.BlockSpec((tm, w), lambda i: (i, 0))
    full = lambda shape: pl.BlockSpec(shape, lambda i: (0,) * len(shape))
    return pl.pallas_call(
        functools.partial(_ml_post_kernel, final=final),
        grid=(n // tm,),
        in_specs=[row(D_INNER), row(D_INNER), row(D_MODEL), full((D_INNER, D_MODEL)),
                  full((1, D_MODEL))],
        out_specs=row(D_MODEL),
        out_shape=jax.ShapeDtypeStruct((n, D_MODEL), F32),
        compiler_params=_params(("parallel",)),
        name="mla_post",
    )(o, g, x, w_out, final_g)


def _rotate_half_cols(w):
    half = QK_ROPE // 2
    return jnp.concatenate([-w[..., half:], w[..., :half]], axis=-1)


def _prep_rwkv(j, rw_in, rw_w0, rw_w2, rw_a0, rw_a2, rw_kk, rw_ka, rw_rk, rw_lnx_g, rw_lnx_b,
               rw_out):
    e = D_INNER
    w = rw_in[j]
    w_main = w[:, :4 * e].astype(BF16)
    ww = w[:, 4 * e:4 * e + 2 * LORA]
    wa = w[:, 4 * e + 2 * LORA:]
    z = jnp.zeros_like(ww)
    w_small = jnp.concatenate([jnp.concatenate([ww, z], axis=1),
                               jnp.concatenate([z, wa], axis=1)], axis=0).astype(BF16)
    return dict(
        w_main=w_main, w_small=w_small, w0=rw_w0[j], w2=rw_w2[j].astype(BF16), a0=rw_a0[j],
        a2=rw_a2[j].astype(BF16), k_k=rw_kk[j][None], k_a=rw_ka[j][None],
        r_k=rw_rk[j].reshape(1, e), lnx_g=rw_lnx_g[j][None], lnx_b=rw_lnx_b[j][None],
        w_out=rw_out[j].astype(BF16))


def _prep_mla(j, ml_in, ml_qn, ml_kvn, ml_uq, ml_ukv, ml_out):
    w = ml_in[j]
    lo = Q_LORA + KV_LORA
    w_kpe = w[:, lo:lo + QK_ROPE]
    w_small = jnp.concatenate([w[:, :lo], w_kpe, _rotate_half_cols(w_kpe)], axis=1).astype(BF16)
    w_g = w[:, lo + QK_ROPE:].astype(BF16)
    uq = ml_uq[j].reshape(Q_LORA, MLA_HEADS, QK_HEAD)
    pe = uq[..., QK_NOPE:]
    wq = jnp.concatenate([uq[..., :QK_NOPE], pe, _rotate_half_cols(pe)], axis=-1)
    wq = wq.reshape(Q_LORA, MLA_HEADS * Q_PAD).astype(BF16)
    ukv = ml_ukv[j].reshape(KV_LORA, MLA_HEADS, QK_NOPE + V_HEAD)
    wkv = jnp.concatenate([ukv[..., :QK_NOPE].reshape(KV_LORA, -1),
                           ukv[..., QK_NOPE:].reshape(KV_LORA, -1)], axis=1).astype(BF16)
    return dict(w_small=w_small, w_g=w_g, qn=ml_qn[j][None], kvn=ml_kvn[j][None], wq=wq,
                wkv=wkv, w_out=ml_out[j].astype(BF16))


def _rope_tables(seq):
    inv_freq = 1.0 / (ROPE_THETA ** (jnp.arange(0, QK_ROPE, 2, dtype=F32) / QK_ROPE))
    ang = jnp.arange(seq, dtype=F32)[:, None] * inv_freq[None, :]
    z = jnp.zeros((seq, LANES - QK_ROPE), F32)
    cs = jnp.concatenate([jnp.cos(ang), jnp.cos(ang), z], axis=1)
    sn = jnp.concatenate([jnp.sin(ang), jnp.sin(ang), z], axis=1)
    return cs, sn


def _trunk(x3, ln_g, final_g, rw, ml):
    batch, seq, _ = x3.shape
    x = x3.reshape(batch * seq, D_MODEL)
    cs, sn = _rope_tables(seq)
    for i in range(DEPTH):
        g = ln_g[i][None]
        if i % 2 == 0:
            p = rw[i // 2]
            xs4, xs2 = _rw_prep(x, g, p["mu"], seq)
            proj = _matmul_streams(xs4, p["w_main"])
            small = _matmul(xs2, p["w_small"])
            yf, yb = _wkv(proj, small, p["w0"], p["w2"], p["a0"], p["a2"], p["k_k"], p["k_a"],
                          batch, seq)
            x = _rw_post(yf, yb, proj, small, x, p["a0"], p["a2"], p["k_a"], p["r_k"],
                         p["lnx_g"], p["lnx_b"], p["w_out"])
        else:
            p = ml[i // 2]
            h = _rmsnorm_bf16(x, g)
            small = _matmul(h, p["w_small"], tn=SMALL_COLS)
            gate = _matmul(h, p["w_g"], out_dtype=BF16)
            q, k, v = _ml_up(small, cs, sn, p["qn"], p["kvn"], p["wq"], p["wkv"], seq)
            o = _attention(q, k, v, batch, seq)
            x = _ml_post(o, gate, x, p["w_out"], final_g[None], final=(i == DEPTH - 1))
    return x.reshape(batch, seq, D_MODEL)


def kernel(x_prompt, x_sample, ln_g, final_g, rw_mu, rw_in, rw_w0, rw_w2, rw_a0, rw_a2, rw_kk,
           rw_ka, rw_rk, rw_lnx_g, rw_lnx_b, rw_out, ml_in, ml_qn, ml_kvn, ml_uq, ml_ukv, ml_out):
    rw = []
    for j in range(rw_in.shape[0]):
        p = _prep_rwkv(j, rw_in, rw_w0, rw_w2, rw_a0, rw_a2, rw_kk, rw_ka, rw_rk, rw_lnx_g,
                       rw_lnx_b, rw_out)
        p["mu"] = rw_mu[j]
        rw.append(p)
    ml = [_prep_mla(j, ml_in, ml_qn, ml_kvn, ml_uq, ml_ukv, ml_out) for j in range(ml_in.shape[0])]
    y_prompt = _trunk(x_prompt, ln_g, final_g, rw, ml)
    y_sample = _trunk(x_sample, ln_g, final_g, rw, ml)
    return (y_prompt, y_sample)
```

```python
import functools

import jax
import jax.numpy as jnp
from jax import lax
from jax.experimental import pallas as pl
from jax.experimental.pallas import tpu as pltpu

F32 = jnp.float32
BF16 = jnp.bfloat16

D_MODEL = 1024
D_INNER = 2048
DEPTH = 4
RWKV_HEAD = 64
RWKV_HEADS = D_INNER // RWKV_HEAD
LORA = 64
LN_X_EPS = 64e-5
NORM_EPS = 1e-6
MLA_HEADS = 16
QK_NOPE = 128
QK_ROPE = 64
V_HEAD = 128
QK_HEAD = QK_NOPE + QK_ROPE
Q_LORA = 384
KV_LORA = 256
ROPE_THETA = 10000.0
Q_PAD = 256
SMALL_COLS = Q_LORA + KV_LORA + 2 * QK_ROPE

LANES = 128
SUBLANES = 8
WKV_CHUNK = 64
WKV_HEADS_PER_STEP = 8
WKV_CHUNKS_PER_STEP = 4
INV_BASE = 16
EXP_NEG_HALF = 0.6065306597126334
LOG2_E = 1.4426950408889634
VMEM_LIMIT = 48 * 1024 * 1024


def _bdot(a, b):
    return jnp.dot(a.astype(BF16), b.astype(BF16), preferred_element_type=F32)


def _bdot_nt(a, b):
    return lax.dot_general(a.astype(BF16), b.astype(BF16), (((1,), (1,)), ((), ())),
                           preferred_element_type=F32)


def _bdot_tn(a, b):
    return lax.dot_general(a.astype(BF16), b.astype(BF16), (((0,), (0,)), ((), ())),
                           preferred_element_type=F32)


def _split2(x):
    hi = x.astype(BF16)
    lo = (x - hi.astype(F32)).astype(BF16)
    return hi, lo


def _dot_x2(x, m_bf16):
    hi, lo = _split2(x)
    return (jnp.dot(hi, m_bf16, preferred_element_type=F32)
            + jnp.dot(lo, m_bf16, preferred_element_type=F32))


def _sigmoid(x):
    return 1.0 / (1.0 + jnp.exp(-x))


def _params(sem):
    return pltpu.CompilerParams(dimension_semantics=sem, vmem_limit_bytes=VMEM_LIMIT)


def _rms(x, g):
    return x * lax.rsqrt(jnp.mean(x * x, axis=-1, keepdims=True) + NORM_EPS) * g


def _rw_prep_kernel(x_ref, xp_ref, xn_ref, g_ref, mu_ref, xs4_ref, xs2_ref, *, tm, seq):
    i = pl.program_id(0)
    g = g_ref[...]
    h = _rms(x_ref[...], g)
    hp = _rms(xp_ref[SUBLANES - 1:SUBLANES, :], g)
    hn = _rms(xn_ref[0:1, :], g)
    row0 = i * tm
    hp = jnp.where((row0 % seq) == 0, 0.0, hp)
    hn = jnp.where(((row0 + tm) % seq) == 0, 0.0, hn)
    rid = lax.broadcasted_iota(jnp.int32, (tm, 1), 0)
    prev = jnp.where(rid == 0, hp, pltpu.roll(h, 1, axis=0))
    nxt = jnp.where(rid == tm - 1, hn, pltpu.roll(h, tm - 1, axis=0))
    xx = 0.5 * (prev + nxt) - h
    mu = mu_ref[...]
    for n, s in enumerate((0, 2, 3, 5)):
        xs4_ref[n] = (h + mu[s:s + 1, :] * xx).astype(BF16)
    xs2_ref[:, :D_MODEL] = (h + mu[1:2, :] * xx).astype(BF16)
    xs2_ref[:, D_MODEL:] = (h + mu[4:5, :] * xx).astype(BF16)


def _rw_prep(x, ln_g, mu, seq, tm=256):
    n = x.shape[0]
    nb8 = n // SUBLANES
    per = tm // SUBLANES
    return pl.pallas_call(
        functools.partial(_rw_prep_kernel, tm=tm, seq=seq),
        grid=(n // tm,),
        in_specs=[
            pl.BlockSpec((tm, D_MODEL), lambda i: (i, 0)),
            pl.BlockSpec((SUBLANES, D_MODEL), lambda i: (jnp.maximum(i * per - 1, 0), 0)),
            pl.BlockSpec((SUBLANES, D_MODEL), lambda i: (jnp.minimum((i + 1) * per, nb8 - 1), 0)),
            pl.BlockSpec((1, D_MODEL), lambda i: (0, 0)),
            pl.BlockSpec((6, D_MODEL), lambda i: (0, 0)),
        ],
        out_specs=[
            pl.BlockSpec((4, tm, D_MODEL), lambda i: (0, i, 0)),
            pl.BlockSpec((tm, 2 * D_MODEL), lambda i: (i, 0)),
        ],
        out_shape=[
            jax.ShapeDtypeStruct((4, n, D_MODEL), BF16),
            jax.ShapeDtypeStruct((n, 2 * D_MODEL), BF16),
        ],
        compiler_params=_params(("parallel",)),
        name="rw_prep",
    )(x, x, x, ln_g, mu)


def _norm_kernel(x_ref, g_ref, o_ref):
    o_ref[...] = _rms(x_ref[...], g_ref[...]).astype(o_ref.dtype)


def _rmsnorm_bf16(x, g, tm=512):
    n = x.shape[0]
    return pl.pallas_call(
        _norm_kernel,
        grid=(n // tm,),
        in_specs=[pl.BlockSpec((tm, D_MODEL), lambda i: (i, 0)),
                  pl.BlockSpec((1, D_MODEL), lambda i: (0, 0))],
        out_specs=pl.BlockSpec((tm, D_MODEL), lambda i: (i, 0)),
        out_shape=jax.ShapeDtypeStruct((n, D_MODEL), BF16),
        compiler_params=_params(("parallel",)),
        name="rmsnorm",
    )(x, g)


def _mm_kernel(x_ref, w_ref, o_ref):
    o_ref[...] = jnp.dot(x_ref[...], w_ref[...], preferred_element_type=F32).astype(o_ref.dtype)


def _matmul(x, w, out_dtype=F32, tm=1024, tn=512):
    n, k = x.shape
    m = w.shape[1]
    tn = min(tn, m)
    return pl.pallas_call(
        _mm_kernel,
        grid=(n // tm, m // tn),
        in_specs=[pl.BlockSpec((tm, k), lambda i, j: (i, 0)),
                  pl.BlockSpec((k, tn), lambda i, j: (0, j))],
        out_specs=pl.BlockSpec((tm, tn), lambda i, j: (i, j)),
        out_shape=jax.ShapeDtypeStruct((n, m), out_dtype),
        compiler_params=_params(("parallel", "parallel")),
        name="matmul",
    )(x, w)


def _matmul_streams(xs, w, tm=2048, tn=512):
    ns, n, k = xs.shape
    per = D_INNER // tn
    return pl.pallas_call(
        _mm_kernel,
        grid=(n // tm, ns, per),
        in_specs=[pl.BlockSpec((None, tm, k), lambda i, s, j: (s, i, 0)),
                  pl.BlockSpec((k, tn), lambda i, s, j: (0, s * per + j))],
        out_specs=pl.BlockSpec((tm, tn), lambda i, s, j: (i, s * per + j)),
        out_shape=jax.ShapeDtypeStruct((n, ns * D_INNER), BF16),
        compiler_params=_params(("parallel", "parallel", "parallel")),
        name="matmul_streams",
    )(xs, w)


def _unit_tri_inverses(mats, row, col, c):
    eye = (row == col).astype(F32)
    same = (row // INV_BASE) == (col // INV_BASE)
    ps = [jnp.where(same, a, 0.0) for a in mats]
    xs = [eye + p for p in ps]
    span = 2
    while span < INV_BASE:
        ps = [_bdot(p, p) for p in ps]
        yield
        xs = [x + _bdot(x, p) for x, p in zip(xs, ps)]
        yield
        span *= 2
    blk = INV_BASE
    while blk < c:
        pair = ((row // (2 * blk)) == (col // (2 * blk))) & jnp.logical_not(
            (row // blk) == (col // blk))
        ts = [_bdot(jnp.where(pair, a, 0.0), x) for a, x in zip(mats, xs)]
        yield
        xs = [x + _bdot(x, t) for x, t in zip(xs, ts)]
        yield
        blk *= 2
    return xs


def _interleave(*gens):
    results = [None] * len(gens)
    live = list(range(len(gens)))
    while live:
        for i in list(live):
            try:
                next(gens[i])
            except StopIteration as stop:
                results[i] = stop.value
                live.remove(i)
    return results


def _wkv_kernel(rf_ref, kf_ref, vf_ref, sf_ref, rb_ref, kb_ref, vb_ref, sb_ref,
                w0_ref, w2_ref, a0_ref, a2_ref, kk_ref, ka_ref,
                yf_ref, yb_ref, state_ref, *, chunk, heads, sub):
    c = chunk
    n = RWKV_HEAD
    width = heads * n
    rows_blk = sub * c

    @pl.when(pl.program_id(2) == 0)
    def _():
        state_ref[...] = jnp.zeros_like(state_ref)

    pc = 2 * c
    row = lax.broadcasted_iota(jnp.int32, (pc, pc), 0)
    col = lax.broadcasted_iota(jnp.int32, (pc, pc), 1)
    same_head = (row // c) == (col // c)
    lane = lax.broadcasted_iota(jnp.int32, (c, LANES), 1)
    head_a = lane < n

    def stack(x, swap=False):
        zero = jnp.zeros_like(x)
        parts = [jnp.where(head_a, x, zero), jnp.where(head_a, zero, x)]
        return jnp.concatenate(parts[::-1] if swap else parts, axis=0)

    lrow = lax.broadcasted_iota(jnp.int32, (LANES, LANES), 0)
    lcol = lax.broadcasted_iota(jnp.int32, (LANES, LANES), 1)
    head_ones = ((lrow // n) == (lcol // n)).astype(BF16)
    brow = lax.broadcasted_iota(jnp.int32, (rows_blk, rows_blk), 0)
    bcol = lax.broadcasted_iota(jnp.int32, (rows_blk, rows_blk), 1)
    same_chunk = (brow // c) == (bcol // c)
    chunk_sel = ((lax.broadcasted_iota(jnp.int32, (rows_blk, LANES), 0) // c)
                 == (lax.broadcasted_iota(jnp.int32, (rows_blk, LANES), 1) // (LANES // sub))
                 ).astype(BF16)
    tn = (((0,), (0,)), ((), ()))

    def chunk_local(us):
        gs = [_bdot_nt(u["lhs"], u["rhs"]) for u in us]
        yield
        x_a = [jnp.concatenate([g[:c, :pc], g[:c, pc:]], axis=0) for g in gs]
        x_r = [jnp.concatenate([g[c:, :pc], g[c:, pc:]], axis=0) for g in gs]
        a_ab = [jnp.where(u["strict"], x, 0.0) for u, x in zip(us, x_a)]
        a_ak = [jnp.where(u["strict_x"], x, 0.0).astype(BF16) for u, x in zip(us, x_a)]
        a_r = [jnp.concatenate([jnp.where(u["incl"], x, 0.0),
                                jnp.where(u["incl_x"], x, 0.0)], axis=1).astype(BF16)
               for u, x in zip(us, x_r)]
        pv = [_bdot(a, u["v"]) for a, u in zip(a_ak, us)]
        yield
        tinv = yield from _unit_tri_inverses(a_ab, row, col, c)
        qs = [_bdot(t, jnp.concatenate([u["at"], p.astype(BF16)], axis=1))
              for t, u, p in zip(tinv, us, pv)]
        yield
        zs = [jnp.concatenate([q.astype(BF16),
                               jnp.concatenate([jnp.zeros((pc, LANES), BF16), u["v"]], axis=1)],
                              axis=0)
              for q, u in zip(qs, us)]
        ry = [_bdot(a, z) for a, z in zip(a_r, zs)]
        yield
        ms = [_bdot_tn(u["end"], z) for u, z in zip(us, zs)]
        return ry, ms

    refs = ((rf_ref, kf_ref, vf_ref, sf_ref), (rb_ref, kb_ref, vb_ref, sb_ref))

    def prepare(d):
        r_ref, k_ref, v_ref, s_ref = refs[d]
        units = []
        tri_s = (row % c > col % c) if d == 0 else (row % c < col % c)
        tri_i = (row % c >= col % c) if d == 0 else (row % c <= col % c)
        other_head = jnp.logical_not(same_head)
        strict, strict_x = same_head & tri_s, other_head & tri_s
        incl, incl_x = same_head & tri_i, other_head & tri_i
        r = r_ref[...].astype(F32)
        k = k_ref[...].astype(F32)
        vb = v_ref[...]
        sm = s_ref[...]
        wl = sm[:, d * LORA:(d + 1) * LORA]
        al = sm[:, (2 + d) * LORA:(3 + d) * LORA]
        w_pre = w0_ref[d:d + 1, :] + _bdot(jnp.tanh(wl), w2_ref[d])
        logw = -EXP_NEG_HALF * _sigmoid(w_pre)
        yield
        a = _sigmoid(a0_ref[d:d + 1, :] + _bdot(al, a2_ref[d]))
        kk = k * kk_ref[...]
        k2 = (kk * kk).astype(BF16)
        ss = jnp.concatenate(
            [jnp.dot(k2[:, o:o + LANES], head_ones, preferred_element_type=F32)
             for o in range(0, width, LANES)], axis=1)
        yield
        kk = kk / jnp.maximum(jnp.sqrt(ss), 1e-12)
        kdir = k * (1.0 + (a - 1.0) * ka_ref[...])
        b = kk * a
        yield
        tri = (same_chunk & ((brow >= bcol) if d == 0 else (brow <= bcol))).astype(BF16)
        l_hi, l_lo = _split2(logw)
        cl = (jnp.dot(tri, l_hi, preferred_element_type=F32)
              + jnp.dot(tri, l_lo, preferred_element_type=F32))
        ends = []
        for ci in range(sub):
            last = ci * c + (c - 1 if d == 0 else 0)
            ends.append(jnp.broadcast_to(cl[last:last + 1, :], (c, width)))
        cl_end = jnp.concatenate(ends, axis=0)
        yield
        rt = (r * jnp.exp(cl)).astype(BF16)
        yield
        at = (-kk * jnp.exp(cl - logw)).astype(BF16)
        yield
        e_neg = jnp.exp(-cl)
        bt = (b * e_neg).astype(BF16)
        kt = (kdir * e_neg).astype(BF16)
        yield
        e_end = jnp.exp(cl_end - cl)
        bh = (b * e_end).astype(BF16)
        kh = (kdir * e_end).astype(BF16)
        yield
        w_end_t = jnp.exp(lax.dot_general(l_hi, chunk_sel, tn, preferred_element_type=F32)
                          + lax.dot_general(l_lo, chunk_sel, tn, preferred_element_type=F32))
        for ci in range(sub):
            rs = slice(ci * c, (ci + 1) * c)
            lane0 = ci * (LANES // sub)
            for p in range(heads // 2):
                ls = slice(p * LANES, (p + 1) * LANES)
                units.append(dict(
                    d=d, ci=ci, p=p, strict=strict, strict_x=strict_x, incl=incl, incl_x=incl_x,
                    at=stack(at[rs, ls]), rt=stack(rt[rs, ls]), v=stack(vb[rs, ls], swap=True),
                    lhs=jnp.concatenate([at[rs, ls], rt[rs, ls]], axis=0),
                    rhs=jnp.concatenate([stack(bt[rs, ls])[:c], stack(kt[rs, ls]),
                                         stack(bt[rs, ls])[c:]], axis=0),
                    end=jnp.concatenate([stack(bh[rs, ls]), stack(kh[rs, ls], swap=True)],
                                        axis=0),
                    w_end=jnp.broadcast_to(w_end_t[ls, lane0:lane0 + 1], (LANES, LANES))))
            yield
        return units

    pairs = heads // 2

    def recur(d, units, local):
        rys, mss = local
        y_ref = (yf_ref, yb_ref)[d]
        by_key = {(u["ci"], u["p"]): i for i, u in enumerate(units)}
        state = [state_ref[d, p] for p in range(pairs)]
        for ci in (range(sub) if d == 0 else range(sub - 1, -1, -1)):
            idx = [by_key[(ci, p)] for p in range(pairs)]
            ys = [_bdot(units[i]["rt"].astype(F32) + rys[i][:, :LANES], s) + rys[i][:, LANES:]
                  for i, s in zip(idx, state)]
            yield
            state = [_bdot(mss[i][:, :LANES], s) + units[i]["w_end"] * s + mss[i][:, LANES:]
                     for i, s in zip(idx, state)]
            y_ref[ci * c:(ci + 1) * c, :] = jnp.concatenate(
                [y[:c] + y[c:] for y in ys], axis=1).astype(y_ref.dtype)
            yield
        for p in range(pairs):
            state_ref[d, p] = state[p]

    units_f, = _interleave(prepare(0))
    local_f, units_b = _interleave(chunk_local(units_f), prepare(1))
    local_b, _ = _interleave(chunk_local(units_b), recur(0, units_f, local_f))
    _interleave(recur(1, units_b, local_b))


def _wkv(proj, small, w0, w2, a0, a2, k_k, k_a, batch, seq):
    n_tok = proj.shape[0]
    hg = WKV_HEADS_PER_STEP
    width = hg * RWKV_HEAD
    rows = WKV_CHUNK * WKV_CHUNKS_PER_STEP
    nb = seq // rows
    groups = D_INNER // width

    def fwd(s):
        return lambda b, g, t: (b * nb + t, s * groups + g)

    def bwd(s):
        return lambda b, g, t: (b * nb + (nb - 1 - t), s * groups + g)

    blk = lambda f: pl.BlockSpec((rows, width), f)
    par = lambda shape: pl.BlockSpec(shape, lambda b, g, t: (0,) * (len(shape) - 1) + (g,))
    return pl.pallas_call(
        functools.partial(_wkv_kernel, chunk=WKV_CHUNK, heads=hg, sub=WKV_CHUNKS_PER_STEP),
        grid=(batch, groups, nb),
        in_specs=[
            blk(fwd(0)), blk(fwd(1)), blk(fwd(2)),
            pl.BlockSpec((rows, 4 * LORA), lambda b, g, t: (b * nb + t, 0)),
            blk(bwd(0)), blk(bwd(1)), blk(bwd(2)),
            pl.BlockSpec((rows, 4 * LORA), lambda b, g, t: (b * nb + (nb - 1 - t), 0)),
            par((2, width)), par((2, LORA, width)), par((2, width)), par((2, LORA, width)),
            par((1, width)), par((1, width)),
        ],
        out_specs=[
            pl.BlockSpec((rows, width), lambda b, g, t: (b * nb + t, g)),
            pl.BlockSpec((rows, width), lambda b, g, t: (b * nb + (nb - 1 - t), g)),
        ],
        out_shape=[jax.ShapeDtypeStruct((n_tok, D_INNER), BF16),
                   jax.ShapeDtypeStruct((n_tok, D_INNER), BF16)],
        scratch_shapes=[pltpu.VMEM((2, hg // 2, LANES, LANES), F32)],
        compiler_params=_params(("parallel", "parallel", "arbitrary")),
        name="wkv7_chunked",
    )(proj, proj, proj, small, proj, proj, proj, small, w0, w2, a0, a2, k_k, k_a)


def _rw_post_kernel(yf_ref, yb_ref, r_ref, k_ref, v_ref, g_ref, s_ref, x_ref,
                    a0_ref, a2_ref, ka_ref, rk_ref, lg_ref, lb_ref, wo_ref, o_ref):
    e = D_INNER
    li = lax.broadcasted_iota(jnp.int32, (LANES, LANES), 0)
    lj = lax.broadcasted_iota(jnp.int32, (LANES, LANES), 1)
    head_ones = ((li // RWKV_HEAD) == (lj // RWKV_HEAD)).astype(BF16)

    def head_sums(z, split):
        dot = _dot_x2 if split else (lambda a, b: jnp.dot(a.astype(BF16), b,
                                                           preferred_element_type=F32))
        return jnp.concatenate([dot(z[:, o:o + LANES], head_ones) for o in range(0, e, LANES)],
                               axis=1)

    y = yf_ref[...].astype(F32) + yb_ref[...].astype(F32)
    yc = y - head_sums(y, True) * (1.0 / RWKV_HEAD)
    var = head_sums(yc * yc, False) * (1.0 / RWKV_HEAD)
    yn = yc * lax.rsqrt(var + LN_X_EPS) * lg_ref[...] + lb_ref[...]
    sm = s_ref[...]
    a_sum = (_sigmoid(a0_ref[0:1, :] + _bdot(sm[:, 2 * LORA:3 * LORA], a2_ref[0]))
             + _sigmoid(a0_ref[1:2, :] + _bdot(sm[:, 3 * LORA:4 * LORA], a2_ref[1])))
    k_sum = k_ref[...].astype(F32) * (2.0 + (a_sum - 2.0) * ka_ref[...])
    coef = head_sums(r_ref[...].astype(F32) * k_sum * rk_ref[...], False)
    g = g_ref[...].astype(F32)
    out = (yn + coef * v_ref[...].astype(F32)) * (g * _sigmoid(g))
    o_ref[...] = x_ref[...] + _bdot(out, wo_ref[...])


def _rw_post(yf, yb, proj, small, x, a0, a2, k_a, r_k, lnx_g, lnx_b, w_out, tm=256):
    n = x.shape[0]
    e = D_INNER
    row = lambda w, j: pl.BlockSpec((tm, w), lambda i: (i, j))
    full = lambda shape: pl.BlockSpec(shape, lambda i: (0,) * len(shape))
    return pl.pallas_call(
        _rw_post_kernel,
        grid=(n // tm,),
        in_specs=[
            row(e, 0), row(e, 0),
            row(e, 0), row(e, 1), row(e, 2), row(e, 3),
            row(4 * LORA, 0), row(D_MODEL, 0),
            full((2, e)), full((2, LORA, e)), full((1, e)), full((1, e)), full((1, e)),
            full((1, e)), full((e, D_MODEL)),
        ],
        out_specs=row(D_MODEL, 0),
        out_shape=jax.ShapeDtypeStruct((n, D_MODEL), F32),
        compiler_params=_params(("parallel",)),
        name="rw_post",
    )(yf, yb, proj, proj, proj, proj, small, x, a0, a2, k_a, r_k, lnx_g, lnx_b, w_out)


def _ml_up_kernel(s_ref, cs_ref, sn_ref, qn_ref, kvn_ref, wq_ref, wkv_ref,
                  q_ref, k_ref, v_ref):
    sm = s_ref[...]
    cs = cs_ref[...]
    sn = sn_ref[...]
    scale = QK_HEAD ** -0.5 * LOG2_E

    def rms(x, g):
        return x * lax.rsqrt(jnp.mean(x * x, axis=-1, keepdims=True) + NORM_EPS) * g

    def rope(x2):
        return x2 * cs + pltpu.roll(x2, QK_ROPE, axis=1) * sn

    cq = rms(sm[:, :Q_LORA], qn_ref[...])
    ckv = rms(sm[:, Q_LORA:Q_LORA + KV_LORA], kvn_ref[...])
    q = _bdot(cq, wq_ref[...])
    kv = _bdot(ckv, wkv_ref[...])
    kpe = rope(sm[:, Q_LORA + KV_LORA:]).astype(BF16)
    lane = lax.broadcasted_iota(jnp.int32, kpe.shape, 1)
    ones_col = (lane == 0).astype(BF16)
    hv = MLA_HEADS * QK_NOPE
    for h in range(MLA_HEADS):
        o = h * Q_PAD
        q_ref[:, o:o + QK_NOPE] = (q[:, o:o + QK_NOPE] * scale).astype(BF16)
        q_ref[:, o + QK_NOPE:o + Q_PAD] = (rope(q[:, o + QK_NOPE:o + Q_PAD]) * scale).astype(BF16)
        k_ref[:, o:o + QK_NOPE] = kv[:, h * QK_NOPE:(h + 1) * QK_NOPE].astype(BF16)
        k_ref[:, o + QK_NOPE:o + Q_PAD] = kpe
        v_ref[:, o:o + V_HEAD] = kv[:, hv + h * V_HEAD:hv + (h + 1) * V_HEAD].astype(BF16)
        v_ref[:, o + V_HEAD:o + Q_PAD] = ones_col


def _ml_up(small, cs, sn, qn, kvn, wq, wkv, seq, tm=256):
    n = small.shape[0]
    per_seq = seq // tm
    row = lambda w: pl.BlockSpec((tm, w), lambda i: (i, 0))
    pos = pl.BlockSpec((tm, LANES), lambda i: (i % per_seq, 0))
    full = lambda shape: pl.BlockSpec(shape, lambda i: (0,) * len(shape))
    hq = MLA_HEADS * Q_PAD
    return pl.pallas_call(
        _ml_up_kernel,
        grid=(n // tm,),
        in_specs=[row(SMALL_COLS), pos, pos, full((1, Q_LORA)), full((1, KV_LORA)),
                  full((Q_LORA, hq)), full((KV_LORA, 2 * MLA_HEADS * V_HEAD))],
        out_specs=[row(hq), row(hq), row(hq)],
        out_shape=[jax.ShapeDtypeStruct((n, hq), BF16)] * 3,
        compiler_params=_params(("parallel",)),
        name="mla_up",
    )(small, cs, sn, qn, kvn, wq, wkv)


def _attn_kernel(q_ref, k_ref, v_ref, o_ref, m_ref, acc_ref, *, tk, parts):
    tq = q_ref.shape[0]
    rows = tq // parts
    nk = k_ref.shape[0] // tk
    nt = (((1,), (1,)), ((), ()))
    sl = [slice(a * rows, (a + 1) * rows) for a in range(parts)]
    lane_tiles = tk // LANES

    def wide(x, reps):
        return jnp.concatenate([x] * reps, axis=1)

    def row_max(s):
        return jnp.broadcast_to(jnp.max(s, axis=1, keepdims=True), (s.shape[0], LANES))

    m_ref[...] = jnp.full_like(m_ref, -jnp.inf)
    acc_ref[...] = jnp.zeros_like(acc_ref)

    def body(j, carry):
        off = pl.multiple_of(j * tk, tk)
        kj = k_ref[pl.ds(off, tk), :]
        vj = v_ref[pl.ds(off, tk), :]
        ss = [lax.dot_general(q_ref[r, :], kj, nt, preferred_element_type=F32) for r in sl]
        for r, s in zip(sl, ss):
            mo = m_ref[r, :]
            mn = jnp.maximum(mo, row_max(s))
            p = jnp.exp2(s - wide(mn, lane_tiles)).astype(BF16)
            acc_ref[r, :] = (acc_ref[r, :] * wide(jnp.exp2(mo - mn), Q_PAD // LANES)
                             + jnp.dot(p, vj, preferred_element_type=F32))
            m_ref[r, :] = mn
        return carry

    lax.fori_loop(0, nk, body, 0, unroll=True)
    acc = acc_ref[...]
    o_ref[...] = (acc[:, :V_HEAD] / acc[:, V_HEAD:V_HEAD + 1]).astype(o_ref.dtype)


def _attention(q, k, v, batch, seq, tq=1024, tk=2048, parts=2):
    n = q.shape[0]
    tq = min(tq, seq)
    tk = min(tk, seq)
    nq = seq // tq
    return pl.pallas_call(
        functools.partial(_attn_kernel, tk=tk, parts=parts),
        grid=(batch, MLA_HEADS, nq),
        in_specs=[
            pl.BlockSpec((tq, Q_PAD), lambda b, h, i: (b * nq + i, h)),
            pl.BlockSpec((seq, Q_PAD), lambda b, h, i: (b, h)),
            pl.BlockSpec((seq, Q_PAD), lambda b, h, i: (b, h)),
        ],
        out_specs=pl.BlockSpec((tq, V_HEAD), lambda b, h, i: (b * nq + i, h)),
        out_shape=jax.ShapeDtypeStruct((n, MLA_HEADS * V_HEAD), BF16),
        scratch_shapes=[pltpu.VMEM((tq, LANES), F32), pltpu.VMEM((tq, Q_PAD), F32)],
        compiler_params=_params(("parallel", "parallel", "parallel")),
        name="mla_flash_attention",
    )(q, k, v)


def _ml_post_kernel(o_ref, g_ref, x_ref, wo_ref, fg_ref, y_ref, *, final):
    g = g_ref[...].astype(F32)
    out = o_ref[...].astype(F32) * (g * _sigmoid(g))
    y = x_ref[...] + _bdot(out, wo_ref[...])
    if final:
        y = _rms(y, fg_ref[...])
    y_ref[...] = y


def _ml_post(o, g, x, w_out, final_g, final, tm=512):
    n = x.shape[0]
    row = lambda w: pl.BlockSpec((tm, w), lambda i: (i, 0))
    full = lambda shape: pl.BlockSpec(shape, lambda i: (0,) * len(shape))
    return pl.pallas_call(
        functools.partial(_ml_post_kernel, final=final),
        grid=(n // tm,),
        in_specs=[row(D_INNER), row(D_INNER), row(D_MODEL), full((D_INNER, D_MODEL)),
                  full((1, D_MODEL))],
        out_specs=row(D_MODEL),
        out_shape=jax.ShapeDtypeStruct((n, D_MODEL), F32),
        compiler_params=_params(("parallel",)),
        name="mla_post",
    )(o, g, x, w_out, final_g)


def _rotate_half_cols(w):
    half = QK_ROPE // 2
    return jnp.concatenate([-w[..., half:], w[..., :half]], axis=-1)


def _prep_rwkv(j, rw_in, rw_w0, rw_w2, rw_a0, rw_a2, rw_kk, rw_ka, rw_rk, rw_lnx_g, rw_lnx_b,
               rw_out):
    e = D_INNER
    w = rw_in[j]
    w_main = w[:, :4 * e].astype(BF16)
    ww = w[:, 4 * e:4 * e + 2 * LORA]
    wa = w[:, 4 * e + 2 * LORA:]
    z = jnp.zeros_like(ww)
    w_small = jnp.concatenate([jnp.concatenate([ww, z], axis=1),
                               jnp.concatenate([z, wa], axis=1)], axis=0).astype(BF16)
    return dict(
        w_main=w_main, w_small=w_small, w0=rw_w0[j], w2=rw_w2[j].astype(BF16), a0=rw_a0[j],
        a2=rw_a2[j].astype(BF16), k_k=rw_kk[j][None], k_a=rw_ka[j][None],
        r_k=rw_rk[j].reshape(1, e), lnx_g=rw_lnx_g[j][None], lnx_b=rw_lnx_b[j][None],
        w_out=rw_out[j].astype(BF16))


def _prep_mla(j, ml_in, ml_qn, ml_kvn, ml_uq, ml_ukv, ml_out):
    w = ml_in[j]
    lo = Q_LORA + KV_LORA
    w_kpe = w[:, lo:lo + QK_ROPE]
    w_small = jnp.concatenate([w[:, :lo], w_kpe, _rotate_half_cols(w_kpe)], axis=1).astype(BF16)
    w_g = w[:, lo + QK_ROPE:].astype(BF16)
    uq = ml_uq[j].reshape(Q_LORA, MLA_HEADS, QK_HEAD)
    pe = uq[..., QK_NOPE:]
    wq = jnp.concatenate([uq[..., :QK_NOPE], pe, _rotate_half_cols(pe)], axis=-1)
    wq = wq.reshape(Q_LORA, MLA_HEADS * Q_PAD).astype(BF16)
    ukv = ml_ukv[j].reshape(KV_LORA, MLA_HEADS, QK_NOPE + V_HEAD)
    wkv = jnp.concatenate([ukv[..., :QK_NOPE].reshape(KV_LORA, -1),
                           ukv[..., QK_NOPE:].reshape(KV_LORA, -1)], axis=1).astype(BF16)
    return dict(w_small=w_small, w_g=w_g, qn=ml_qn[j][None], kvn=ml_kvn[j][None], wq=wq,
                wkv=wkv, w_out=ml_out[j].astype(BF16))


def _rope_tables(seq):
    inv_freq = 1.0 / (ROPE_THETA ** (jnp.arange(0, QK_ROPE, 2, dtype=F32) / QK_ROPE))
    ang = jnp.arange(seq, dtype=F32)[:, None] * inv_freq[None, :]
    z = jnp.zeros((seq, LANES - QK_ROPE), F32)
    cs = jnp.concatenate([jnp.cos(ang), jnp.cos(ang), z], axis=1)
    sn = jnp.concatenate([jnp.sin(ang), jnp.sin(ang), z], axis=1)
    return cs, sn


def _trunk(x3, ln_g, final_g, rw, ml):
    batch, seq, _ = x3.shape
    x = x3.reshape(batch * seq, D_MODEL)
    cs, sn = _rope_tables(seq)
    for i in range(DEPTH):
        g = ln_g[i][None]
        if i % 2 == 0:
            p = rw[i // 2]
            xs4, xs2 = _rw_prep(x, g, p["mu"], seq)
            proj = _matmul_streams(xs4, p["w_main"])
            small = _matmul(xs2, p["w_small"])
            yf, yb = _wkv(proj, small, p["w0"], p["w2"], p["a0"], p["a2"], p["k_k"], p["k_a"],
                          batch, seq)
            x = _rw_post(yf, yb, proj, small, x, p["a0"], p["a2"], p["k_a"], p["r_k"],
                         p["lnx_g"], p["lnx_b"], p["w_out"])
        else:
            p = ml[i // 2]
            h = _rmsnorm_bf16(x, g)
            small = _matmul(h, p["w_small"], tn=SMALL_COLS)
            gate = _matmul(h, p["w_g"], out_dtype=BF16)
            q, k, v = _ml_up(small, cs, sn, p["qn"], p["kvn"], p["wq"], p["wkv"], seq)
            o = _attention(q, k, v, batch, seq)
            x = _ml_post(o, gate, x, p["w_out"], final_g[None], final=(i == DEPTH - 1))
    return x.reshape(batch, seq, D_MODEL)


def kernel(x_prompt, x_sample, ln_g, final_g, rw_mu, rw_in, rw_w0, rw_w2, rw_a0, rw_a2, rw_kk,
           rw_ka, rw_rk, rw_lnx_g, rw_lnx_b, rw_out, ml_in, ml_qn, ml_kvn, ml_uq, ml_ukv, ml_out):
    rw = []
    for j in range(rw_in.shape[0]):
        p = _prep_rwkv(j, rw_in, rw_w0, rw_w2, rw_a0, rw_a2, rw_kk, rw_ka, rw_rk, rw_lnx_g,
                       rw_lnx_b, rw_out)
        p["mu"] = rw_mu[j]
        rw.append(p)
    ml = [_prep_mla(j, ml_in, ml_qn, ml_kvn, ml_uq, ml_ukv, ml_out) for j in range(ml_in.shape[0])]
    y_prompt = _trunk(x_prompt, ln_g, final_g, rw, ml)
    y_sample = _trunk(x_sample, ln_g, final_g, rw, ml)
    return (y_prompt, y_sample)
```

```python
import functools

import jax
import jax.numpy as jnp
from jax import lax
from jax.experimental import pallas as pl
from jax.experimental.pallas import tpu as pltpu

F32 = jnp.float32
BF16 = jnp.bfloat16

D_MODEL = 1024
D_INNER = 2048
DEPTH = 4
RWKV_HEAD = 64
RWKV_HEADS = D_INNER // RWKV_HEAD
LORA = 64
LN_X_EPS = 64e-5
NORM_EPS = 1e-6
MLA_HEADS = 16
QK_NOPE = 128
QK_ROPE = 64
V_HEAD = 128
QK_HEAD = QK_NOPE + QK_ROPE
Q_LORA = 384
KV_LORA = 256
ROPE_THETA = 10000.0
Q_PAD = 256
SMALL_COLS = Q_LORA + KV_LORA + 2 * QK_ROPE

LANES = 128
SUBLANES = 8
WKV_CHUNK = 64
WKV_HEADS_PER_STEP = 8
WKV_CHUNKS_PER_STEP = 4
INV_BASE = 16
EXP_NEG_HALF = 0.6065306597126334
LOG2_E = 1.4426950408889634
VMEM_LIMIT = 48 * 1024 * 1024


def _bdot(a, b):
    return jnp.dot(a.astype(BF16), b.astype(BF16), preferred_element_type=F32)


def _bdot_nt(a, b):
    return lax.dot_general(a.astype(BF16), b.astype(BF16), (((1,), (1,)), ((), ())),
                           preferred_element_type=F32)


def _bdot_tn(a, b):
    return lax.dot_general(a.astype(BF16), b.astype(BF16), (((0,), (0,)), ((), ())),
                           preferred_element_type=F32)


def _split2(x):
    hi = x.astype(BF16)
    lo = (x - hi.astype(F32)).astype(BF16)
    return hi, lo


def _dot_x2(x, m_bf16):
    hi, lo = _split2(x)
    return (jnp.dot(hi, m_bf16, preferred_element_type=F32)
            + jnp.dot(lo, m_bf16, preferred_element_type=F32))


def _sigmoid(x):
    return 1.0 / (1.0 + jnp.exp(-x))


def _params(sem):
    return pltpu.CompilerParams(dimension_semantics=sem, vmem_limit_bytes=VMEM_LIMIT)


def _rms(x, g):
    return x * lax.rsqrt(jnp.mean(x * x, axis=-1, keepdims=True) + NORM_EPS) * g


def _rw_prep_kernel(x_ref, xp_ref, xn_ref, g_ref, mu_ref, xs4_ref, xs2_ref, *, tm, seq):
    i = pl.program_id(0)
    g = g_ref[...]
    h = _rms(x_ref[...], g)
    hp = _rms(xp_ref[SUBLANES - 1:SUBLANES, :], g)
    hn = _rms(xn_ref[0:1, :], g)
    row0 = i * tm
    hp = jnp.where((row0 % seq) == 0, 0.0, hp)
    hn = jnp.where(((row0 + tm) % seq) == 0, 0.0, hn)
    rid = lax.broadcasted_iota(jnp.int32, (tm, 1), 0)
    prev = jnp.where(rid == 0, hp, pltpu.roll(h, 1, axis=0))
    nxt = jnp.where(rid == tm - 1, hn, pltpu.roll(h, tm - 1, axis=0))
    xx = 0.5 * (prev + nxt) - h
    mu = mu_ref[...]
    for n, s in enumerate((0, 2, 3, 5)):
        xs4_ref[n] = (h + mu[s:s + 1, :] * xx).astype(BF16)
    xs2_ref[:, :D_MODEL] = (h + mu[1:2, :] * xx).astype(BF16)
    xs2_ref[:, D_MODEL:] = (h + mu[4:5, :] * xx).astype(BF16)


def _rw_prep(x, ln_g, mu, seq, tm=256):
    n = x.shape[0]
    nb8 = n // SUBLANES
    per = tm // SUBLANES
    return pl.pallas_call(
        functools.partial(_rw_prep_kernel, tm=tm, seq=seq),
        grid=(n // tm,),
        in_specs=[
            pl.BlockSpec((tm, D_MODEL), lambda i: (i, 0)),
            pl.BlockSpec((SUBLANES, D_MODEL), lambda i: (jnp.maximum(i * per - 1, 0), 0)),
            pl.BlockSpec((SUBLANES, D_MODEL), lambda i: (jnp.minimum((i + 1) * per, nb8 - 1), 0)),
            pl.BlockSpec((1, D_MODEL), lambda i: (0, 0)),
            pl.BlockSpec((6, D_MODEL), lambda i: (0, 0)),
        ],
        out_specs=[
            pl.BlockSpec((4, tm, D_MODEL), lambda i: (0, i, 0)),
            pl.BlockSpec((tm, 2 * D_MODEL), lambda i: (i, 0)),
        ],
        out_shape=[
            jax.ShapeDtypeStruct((4, n, D_MODEL), BF16),
            jax.ShapeDtypeStruct((n, 2 * D_MODEL), BF16),
        ],
        compiler_params=_params(("parallel",)),
        name="rw_prep",
    )(x, x, x, ln_g, mu)


def _norm_kernel(x_ref, g_ref, o_ref):
    o_ref[...] = _rms(x_ref[...], g_ref[...]).astype(o_ref.dtype)


def _rmsnorm_bf16(x, g, tm=512):
    n = x.shape[0]
    return pl.pallas_call(
        _norm_kernel,
        grid=(n // tm,),
        in_specs=[pl.BlockSpec((tm, D_MODEL), lambda i: (i, 0)),
                  pl.BlockSpec((1, D_MODEL), lambda i: (0, 0))],
        out_specs=pl.BlockSpec((tm, D_MODEL), lambda i: (i, 0)),
        out_shape=jax.ShapeDtypeStruct((n, D_MODEL), BF16),
        compiler_params=_params(("parallel",)),
        name="rmsnorm",
    )(x, g)


def _mm_kernel(x_ref, w_ref, o_ref):
    o_ref[...] = jnp.dot(x_ref[...], w_ref[...], preferred_element_type=F32).astype(o_ref.dtype)


def _matmul(x, w, out_dtype=F32, tm=1024, tn=512):
    n, k = x.shape
    m = w.shape[1]
    tn = min(tn, m)
    return pl.pallas_call(
        _mm_kernel,
        grid=(n // tm, m // tn),
        in_specs=[pl.BlockSpec((tm, k), lambda i, j: (i, 0)),
                  pl.BlockSpec((k, tn), lambda i, j: (0, j))],
        out_specs=pl.BlockSpec((tm, tn), lambda i, j: (i, j)),
        out_shape=jax.ShapeDtypeStruct((n, m), out_dtype),
        compiler_params=_params(("parallel", "parallel")),
        name="matmul",
    )(x, w)


def _norm_mm_kernel(x_ref, g_ref, w_ref, o_ref):
    h = _rms(x_ref[...], g_ref[...]).astype(BF16)
    o_ref[...] = jnp.dot(h, w_ref[...], preferred_element_type=F32).astype(o_ref.dtype)


def _norm_matmul(x, g, w, out_dtype=F32, tm=1024, tn=512):
    n, k = x.shape
    m = w.shape[1]
    tn = min(tn, m)
    return pl.pallas_call(
        _norm_mm_kernel,
        grid=(n // tm, m // tn),
        in_specs=[pl.BlockSpec((tm, k), lambda i, j: (i, 0)),
                  pl.BlockSpec((1, k), lambda i, j: (0, 0)),
                  pl.BlockSpec((k, tn), lambda i, j: (0, j))],
        out_specs=pl.BlockSpec((tm, tn), lambda i, j: (i, j)),
        out_shape=jax.ShapeDtypeStruct((n, m), out_dtype),
        compiler_params=_params(("parallel", "parallel")),
        name="norm_matmul",
    )(x, g, w)


def _matmul_streams(xs, w, tm=2048, tn=512):
    ns, n, k = xs.shape
    per = D_INNER // tn
    return pl.pallas_call(
        _mm_kernel,
        grid=(n // tm, ns, per),
        in_specs=[pl.BlockSpec((None, tm, k), lambda i, s, j: (s, i, 0)),
                  pl.BlockSpec((k, tn), lambda i, s, j: (0, s * per + j))],
        out_specs=pl.BlockSpec((tm, tn), lambda i, s, j: (i, s * per + j)),
        out_shape=jax.ShapeDtypeStruct((n, ns * D_INNER), BF16),
        compiler_params=_params(("parallel", "parallel", "parallel")),
        name="matmul_streams",
    )(xs, w)


def _unit_tri_inverses(mats, row, col, c):
    eye = (row == col).astype(F32)
    same = (row // INV_BASE) == (col // INV_BASE)
    ps = [jnp.where(same, a, 0.0) for a in mats]
    xs = [eye + p for p in ps]
    span = 2
    while span < INV_BASE:
        ps = [_bdot(p, p) for p in ps]
        yield
        xs = [x + _bdot(x, p) for x, p in zip(xs, ps)]
        yield
        span *= 2
    blk = INV_BASE
    while blk < c:
        pair = ((row // (2 * blk)) == (col // (2 * blk))) & jnp.logical_not(
            (row // blk) == (col // blk))
        ts = [_bdot(jnp.where(pair, a, 0.0), x) for a, x in zip(mats, xs)]
        yield
        xs = [x + _bdot(x, t) for x, t in zip(xs, ts)]
        yield
        blk *= 2
    return xs


def _interleave(*gens):
    results = [None] * len(gens)
    live = list(range(len(gens)))
    while live:
        for i in list(live):
            try:
                next(gens[i])
            except StopIteration as stop:
                results[i] = stop.value
                live.remove(i)
    return results


def _wkv_kernel(rf_ref, kf_ref, vf_ref, sf_ref, rb_ref, kb_ref, vb_ref, sb_ref,
                w0_ref, w2_ref, a0_ref, a2_ref, kk_ref, ka_ref,
                yf_ref, yb_ref, state_ref, *, chunk, heads, sub):
    c = chunk
    n = RWKV_HEAD
    width = heads * n
    rows_blk = sub * c

    @pl.when(pl.program_id(2) == 0)
    def _():
        state_ref[...] = jnp.zeros_like(state_ref)

    pc = 2 * c
    row = lax.broadcasted_iota(jnp.int32, (pc, pc), 0)
    col = lax.broadcasted_iota(jnp.int32, (pc, pc), 1)
    same_head = (row // c) == (col // c)
    lane = lax.broadcasted_iota(jnp.int32, (c, LANES), 1)
    head_a = lane < n

    def stack(x, swap=False):
        zero = jnp.zeros_like(x)
        parts = [jnp.where(head_a, x, zero), jnp.where(head_a, zero, x)]
        return jnp.concatenate(parts[::-1] if swap else parts, axis=0)

    lrow = lax.broadcasted_iota(jnp.int32, (LANES, LANES), 0)
    lcol = lax.broadcasted_iota(jnp.int32, (LANES, LANES), 1)
    head_ones = ((lrow // n) == (lcol // n)).astype(BF16)
    brow = lax.broadcasted_iota(jnp.int32, (rows_blk, rows_blk), 0)
    bcol = lax.broadcasted_iota(jnp.int32, (rows_blk, rows_blk), 1)
    same_chunk = (brow // c) == (bcol // c)
    chunk_sel = ((lax.broadcasted_iota(jnp.int32, (rows_blk, LANES), 0) // c)
                 == (lax.broadcasted_iota(jnp.int32, (rows_blk, LANES), 1) // (LANES // sub))
                 ).astype(BF16)
    tn = (((0,), (0,)), ((), ()))

    def chunk_local(us):
        gs = [_bdot_nt(u["lhs"], u["rhs"]) for u in us]
        yield
        x_a = [jnp.concatenate([g[:c, :pc], g[:c, pc:]], axis=0) for g in gs]
        x_r = [jnp.concatenate([g[c:, :pc], g[c:, pc:]], axis=0) for g in gs]
        a_ab = [jnp.where(u["strict"], x, 0.0) for u, x in zip(us, x_a)]
        a_ak = [jnp.where(u["strict_x"], x, 0.0).astype(BF16) for u, x in zip(us, x_a)]
        a_r = [jnp.concatenate([jnp.where(u["incl"], x, 0.0),
                                jnp.where(u["incl_x"], x, 0.0)], axis=1).astype(BF16)
               for u, x in zip(us, x_r)]
        pv = [_bdot(a, u["v"]) for a, u in zip(a_ak, us)]
        yield
        tinv = yield from _unit_tri_inverses(a_ab, row, col, c)
        qs = [_bdot(t, jnp.concatenate([u["at"], p.astype(BF16)], axis=1))
              for t, u, p in zip(tinv, us, pv)]
        yield
        zs = [jnp.concatenate([q.astype(BF16),
                               jnp.concatenate([jnp.zeros((pc, LANES), BF16), u["v"]], axis=1)],
                              axis=0)
              for q, u in zip(qs, us)]
        ry = [_bdot(a, z) for a, z in zip(a_r, zs)]
        yield
        ms = [_bdot_tn(u["end"], z) for u, z in zip(us, zs)]
        return ry, ms

    refs = ((rf_ref, kf_ref, vf_ref, sf_ref), (rb_ref, kb_ref, vb_ref, sb_ref))

    def prepare(d):
        r_ref, k_ref, v_ref, s_ref = refs[d]
        units = []
        tri_s = (row % c > col % c) if d == 0 else (row % c < col % c)
        tri_i = (row % c >= col % c) if d == 0 else (row % c <= col % c)
        other_head = jnp.logical_not(same_head)
        strict, strict_x = same_head & tri_s, other_head & tri_s
        incl, incl_x = same_head & tri_i, other_head & tri_i
        r = r_ref[...].astype(F32)
        k = k_ref[...].astype(F32)
        vb = v_ref[...]
        sm = s_ref[...]
        wl = sm[:, d * LORA:(d + 1) * LORA]
        al = sm[:, (2 + d) * LORA:(3 + d) * LORA]
        w_pre = w0_ref[d:d + 1, :] + _bdot(jnp.tanh(wl), w2_ref[d])
        logw = -EXP_NEG_HALF * _sigmoid(w_pre)
        yield
        a = _sigmoid(a0_ref[d:d + 1, :] + _bdot(al, a2_ref[d]))
        kk = k * kk_ref[...]
        k2 = (kk * kk).astype(BF16)
        ss = jnp.concatenate(
            [jnp.dot(k2[:, o:o + LANES], head_ones, preferred_element_type=F32)
             for o in range(0, width, LANES)], axis=1)
        yield
        kk = kk / jnp.maximum(jnp.sqrt(ss), 1e-12)
        kdir = k * (1.0 + (a - 1.0) * ka_ref[...])
        b = kk * a
        yield
        tri = (same_chunk & ((brow >= bcol) if d == 0 else (brow <= bcol))).astype(BF16)
        l_hi, l_lo = _split2(logw)
        cl = (jnp.dot(tri, l_hi, preferred_element_type=F32)
              + jnp.dot(tri, l_lo, preferred_element_type=F32))
        ends = []
        for ci in range(sub):
            last = ci * c + (c - 1 if d == 0 else 0)
            ends.append(jnp.broadcast_to(cl[last:last + 1, :], (c, width)))
        cl_end = jnp.concatenate(ends, axis=0)
        yield
        rt = (r * jnp.exp(cl)).astype(BF16)
        yield
        at = (-kk * jnp.exp(cl - logw)).astype(BF16)
        yield
        e_neg = jnp.exp(-cl)
        bt = (b * e_neg).astype(BF16)
        kt = (kdir * e_neg).astype(BF16)
        yield
        e_end = jnp.exp(cl_end - cl)
        bh = (b * e_end).astype(BF16)
        kh = (kdir * e_end).astype(BF16)
        yield
        w_end_t = jnp.exp(lax.dot_general(l_hi, chunk_sel, tn, preferred_element_type=F32)
                          + lax.dot_general(l_lo, chunk_sel, tn, preferred_element_type=F32))
        for ci in range(sub):
            rs = slice(ci * c, (ci + 1) * c)
            lane0 = ci * (LANES // sub)
            for p in range(heads // 2):
                ls = slice(p * LANES, (p + 1) * LANES)
                units.append(dict(
                    d=d, ci=ci, p=p, strict=strict, strict_x=strict_x, incl=incl, incl_x=incl_x,
                    at=stack(at[rs, ls]), rt=stack(rt[rs, ls]), v=stack(vb[rs, ls], swap=True),
                    lhs=jnp.concatenate([at[rs, ls], rt[rs, ls]], axis=0),
                    rhs=jnp.concatenate([stack(bt[rs, ls])[:c], stack(kt[rs, ls]),
                                         stack(bt[rs, ls])[c:]], axis=0),
                    end=jnp.concatenate([stack(bh[rs, ls]), stack(kh[rs, ls], swap=True)],
                                        axis=0),
                    w_end=jnp.broadcast_to(w_end_t[ls, lane0:lane0 + 1], (LANES, LANES))))
            yield
        return units

    pairs = heads // 2

    def recur(d, units, local):
        rys, mss = local
        y_ref = (yf_ref, yb_ref)[d]
        by_key = {(u["ci"], u["p"]): i for i, u in enumerate(units)}
        state = [state_ref[d, p] for p in range(pairs)]
        for ci in (range(sub) if d == 0 else range(sub - 1, -1, -1)):
            idx = [by_key[(ci, p)] for p in range(pairs)]
            ys = [_bdot(units[i]["rt"].astype(F32) + rys[i][:, :LANES], s) + rys[i][:, LANES:]
                  for i, s in zip(idx, state)]
            yield
            state = [_bdot(mss[i][:, :LANES], s) + units[i]["w_end"] * s + mss[i][:, LANES:]
                     for i, s in zip(idx, state)]
            y_ref[ci * c:(ci + 1) * c, :] = jnp.concatenate(
                [y[:c] + y[c:] for y in ys], axis=1).astype(y_ref.dtype)
            yield
        for p in range(pairs):
            state_ref[d, p] = state[p]

    units_f, = _interleave(prepare(0))
    local_f, units_b = _interleave(chunk_local(units_f), prepare(1))
    local_b, _ = _interleave(chunk_local(units_b), recur(0, units_f, local_f))
    _interleave(recur(1, units_b, local_b))


def _wkv(proj, small, w0, w2, a0, a2, k_k, k_a, batch, seq):
    n_tok = proj.shape[0]
    hg = WKV_HEADS_PER_STEP
    width = hg * RWKV_HEAD
    rows = WKV_CHUNK * WKV_CHUNKS_PER_STEP
    nb = seq // rows
    groups = D_INNER // width

    def fwd(s):
        return lambda b, g, t: (b * nb + t, s * groups + g)

    def bwd(s):
        return lambda b, g, t: (b * nb + (nb - 1 - t), s * groups + g)

    blk = lambda f: pl.BlockSpec((rows, width), f)
    par = lambda shape: pl.BlockSpec(shape, lambda b, g, t: (0,) * (len(shape) - 1) + (g,))
    return pl.pallas_call(
        functools.partial(_wkv_kernel, chunk=WKV_CHUNK, heads=hg, sub=WKV_CHUNKS_PER_STEP),
        grid=(batch, groups, nb),
        in_specs=[
            blk(fwd(0)), blk(fwd(1)), blk(fwd(2)),
            pl.BlockSpec((rows, 4 * LORA), lambda b, g, t: (b * nb + t, 0)),
            blk(bwd(0)), blk(bwd(1)), blk(bwd(2)),
            pl.BlockSpec((rows, 4 * LORA), lambda b, g, t: (b * nb + (nb - 1 - t), 0)),
            par((2, width)), par((2, LORA, width)), par((2, width)), par((2, LORA, width)),
            par((1, width)), par((1, width)),
        ],
        out_specs=[
            pl.BlockSpec((rows, width), lambda b, g, t: (b * nb + t, g)),
            pl.BlockSpec((rows, width), lambda b, g, t: (b * nb + (nb - 1 - t), g)),
        ],
        out_shape=[jax.ShapeDtypeStruct((n_tok, D_INNER), BF16),
                   jax.ShapeDtypeStruct((n_tok, D_INNER), BF16)],
        scratch_shapes=[pltpu.VMEM((2, hg // 2, LANES, LANES), F32)],
        compiler_params=_params(("parallel", "parallel", "arbitrary")),
        name="wkv7_chunked",
    )(proj, proj, proj, small, proj, proj, proj, small, w0, w2, a0, a2, k_k, k_a)


def _rw_post_kernel(yf_ref, yb_ref, r_ref, k_ref, v_ref, g_ref, s_ref, x_ref,
                    a0_ref, a2_ref, ka_ref, rk_ref, lg_ref, lb_ref, wo_ref, o_ref):
    e = D_INNER
    li = lax.broadcasted_iota(jnp.int32, (LANES, LANES), 0)
    lj = lax.broadcasted_iota(jnp.int32, (LANES, LANES), 1)
    head_ones = ((li // RWKV_HEAD) == (lj // RWKV_HEAD)).astype(BF16)

    def head_sums(z, split):
        dot = _dot_x2 if split else (lambda a, b: jnp.dot(a.astype(BF16), b,
                                                           preferred_element_type=F32))
        return jnp.concatenate([dot(z[:, o:o + LANES], head_ones) for o in range(0, e, LANES)],
                               axis=1)

    y = yf_ref[...].astype(F32) + yb_ref[...].astype(F32)
    yc = y - head_sums(y, True) * (1.0 / RWKV_HEAD)
    var = head_sums(yc * yc, False) * (1.0 / RWKV_HEAD)
    yn = yc * lax.rsqrt(var + LN_X_EPS) * lg_ref[...] + lb_ref[...]
    sm = s_ref[...]
    a_sum = (_sigmoid(a0_ref[0:1, :] + _bdot(sm[:, 2 * LORA:3 * LORA], a2_ref[0]))
             + _sigmoid(a0_ref[1:2, :] + _bdot(sm[:, 3 * LORA:4 * LORA], a2_ref[1])))
    k_sum = k_ref[...].astype(F32) * (2.0 + (a_sum - 2.0) * ka_ref[...])
    coef = head_sums(r_ref[...].astype(F32) * k_sum * rk_ref[...], False)
    g = g_ref[...].astype(F32)
    out = (yn + coef * v_ref[...].astype(F32)) * (g * _sigmoid(g))
    o_ref[...] = x_ref[...] + _bdot(out, wo_ref[...])


def _rw_post(yf, yb, proj, small, x, a0, a2, k_a, r_k, lnx_g, lnx_b, w_out, tm=256):
    n = x.shape[0]
    e = D_INNER
    row = lambda w, j: pl.BlockSpec((tm, w), lambda i: (i, j))
    full = lambda shape: pl.BlockSpec(shape, lambda i: (0,) * len(shape))
    return pl.pallas_call(
        _rw_post_kernel,
        grid=(n // tm,),
        in_specs=[
            row(e, 0), row(e, 0),
            row(e, 0), row(e, 1), row(e, 2), row(e, 3),
            row(4 * LORA, 0), row(D_MODEL, 0),
            full((2, e)), full((2, LORA, e)), full((1, e)), full((1, e)), full((1, e)),
            full((1, e)), full((e, D_MODEL)),
        ],
        out_specs=row(D_MODEL, 0),
        out_shape=jax.ShapeDtypeStruct((n, D_MODEL), F32),
        compiler_params=_params(("parallel",)),
        name="rw_post",
    )(yf, yb, proj, proj, proj, proj, small, x, a0, a2, k_a, r_k, lnx_g, lnx_b, w_out)


def _ml_up_kernel(s_ref, cs_ref, sn_ref, qn_ref, kvn_ref, wq_ref, wkv_ref,
                  q_ref, k_ref, v_ref):
    sm = s_ref[...]
    cs = cs_ref[...]
    sn = sn_ref[...]
    scale = QK_HEAD ** -0.5 * LOG2_E

    def rms(x, g):
        return x * lax.rsqrt(jnp.mean(x * x, axis=-1, keepdims=True) + NORM_EPS) * g

    def rope(x2):
        return x2 * cs + pltpu.roll(x2, QK_ROPE, axis=1) * sn

    cq = rms(sm[:, :Q_LORA], qn_ref[...])
    ckv = rms(sm[:, Q_LORA:Q_LORA + KV_LORA], kvn_ref[...])
    q = _bdot(cq, wq_ref[...])
    kv = _bdot(ckv, wkv_ref[...])
    kpe = rope(sm[:, Q_LORA + KV_LORA:]).astype(BF16)
    lane = lax.broadcasted_iota(jnp.int32, kpe.shape, 1)
    ones_col = (lane == 0).astype(BF16)
    hv = MLA_HEADS * QK_NOPE
    for h in range(MLA_HEADS):
        o = h * Q_PAD
        q_ref[:, o:o + QK_NOPE] = (q[:, o:o + QK_NOPE] * scale).astype(BF16)
        q_ref[:, o + QK_NOPE:o + Q_PAD] = (rope(q[:, o + QK_NOPE:o + Q_PAD]) * scale).astype(BF16)
        k_ref[:, o:o + QK_NOPE] = kv[:, h * QK_NOPE:(h + 1) * QK_NOPE].astype(BF16)
        k_ref[:, o + QK_NOPE:o + Q_PAD] = kpe
        v_ref[:, o:o + V_HEAD] = kv[:, hv + h * V_HEAD:hv + (h + 1) * V_HEAD].astype(BF16)
        v_ref[:, o + V_HEAD:o + Q_PAD] = ones_col


def _ml_up(small, cs, sn, qn, kvn, wq, wkv, seq, tm=256):
    n = small.shape[0]
    per_seq = seq // tm
    row = lambda w: pl.BlockSpec((tm, w), lambda i: (i, 0))
    pos = pl.BlockSpec((tm, LANES), lambda i: (i % per_seq, 0))
    full = lambda shape: pl.BlockSpec(shape, lambda i: (0,) * len(shape))
    hq = MLA_HEADS * Q_PAD
    return pl.pallas_call(
        _ml_up_kernel,
        grid=(n // tm,),
        in_specs=[row(SMALL_COLS), pos, pos, full((1, Q_LORA)), full((1, KV_LORA)),
                  full((Q_LORA, hq)), full((KV_LORA, 2 * MLA_HEADS * V_HEAD))],
        out_specs=[row(hq), row(hq), row(hq)],
        out_shape=[jax.ShapeDtypeStruct((n, hq), BF16)] * 3,
        compiler_params=_params(("parallel",)),
        name="mla_up",
    )(small, cs, sn, qn, kvn, wq, wkv)


def _attn_kernel(q_ref, k_ref, v_ref, o_ref, m_ref, acc_ref, *, tk, parts):
    tq = q_ref.shape[0]
    rows = tq // parts
    nk = k_ref.shape[0] // tk
    nt = (((1,), (1,)), ((), ()))
    sl = [slice(a * rows, (a + 1) * rows) for a in range(parts)]
    lane_tiles = tk // LANES

    def wide(x, reps):
        return jnp.concatenate([x] * reps, axis=1)

    def row_max(s):
        return jnp.broadcast_to(jnp.max(s, axis=1, keepdims=True), (s.shape[0], LANES))

    m_ref[...] = jnp.full_like(m_ref, -jnp.inf)
    acc_ref[...] = jnp.zeros_like(acc_ref)

    def body(j, carry):
        off = pl.multiple_of(j * tk, tk)
        kj = k_ref[pl.ds(off, tk), :]
        vj = v_ref[pl.ds(off, tk), :]
        ss = [lax.dot_general(q_ref[r, :], kj, nt, preferred_element_type=F32) for r in sl]
        for r, s in zip(sl, ss):
            mo = m_ref[r, :]
            mn = jnp.maximum(mo, row_max(s))
            p = jnp.exp2(s - wide(mn, lane_tiles)).astype(BF16)
            acc_ref[r, :] = (acc_ref[r, :] * wide(jnp.exp2(mo - mn), Q_PAD // LANES)
                             + jnp.dot(p, vj, preferred_element_type=F32))
            m_ref[r, :] = mn
        return carry

    lax.fori_loop(0, nk, body, 0, unroll=True)
    acc = acc_ref[...]
    o_ref[...] = (acc[:, :V_HEAD] / acc[:, V_HEAD:V_HEAD + 1]).astype(o_ref.dtype)


def _attention(q, k, v, batch, seq, tq=1024, tk=2048, parts=2):
    n = q.shape[0]
    tq = min(tq, seq)
    tk = min(tk, seq)
    nq = seq // tq
    return pl.pallas_call(
        functools.partial(_attn_kernel, tk=tk, parts=parts),
        grid=(batch, MLA_HEADS, nq),
        in_specs=[
            pl.BlockSpec((tq, Q_PAD), lambda b, h, i: (b * nq + i, h)),
            pl.BlockSpec((seq, Q_PAD), lambda b, h, i: (b, h)),
            pl.BlockSpec((seq, Q_PAD), lambda b, h, i: (b, h)),
        ],
        out_specs=pl.BlockSpec((tq, V_HEAD), lambda b, h, i: (b * nq + i, h)),
        out_shape=jax.ShapeDtypeStruct((n, MLA_HEADS * V_HEAD), BF16),
        scratch_shapes=[pltpu.VMEM((tq, LANES), F32), pltpu.VMEM((tq, Q_PAD), F32)],
        compiler_params=_params(("parallel", "parallel", "parallel")),
        name="mla_flash_attention",
    )(q, k, v)


def _ml_post_kernel(o_ref, g_ref, x_ref, wo_ref, fg_ref, y_ref, *, final):
    g = g_ref[...].astype(F32)
    out = o_ref[...].astype(F32) * (g * _sigmoid(g))
    y = x_ref[...] + _bdot(out, wo_ref[...])
    if final:
        y = _rms(y, fg_ref[...])
    y_ref[...] = y


def _ml_post(o, g, x, w_out, final_g, final, tm=512):
    n = x.shape[0]
    row = lambda w: pl.BlockSpec((tm, w), lambda i: (i, 0))
    full = lambda shape: pl.BlockSpec(shape, lambda i: (0,) * len(shape))
    return pl.pallas_call(
        functools.partial(_ml_post_kernel, final=final),
        grid=(n // tm,),
        in_specs=[row(D_INNER), row(D_INNER), row(D_MODEL), full((D_INNER, D_MODEL)),
                  full((1, D_MODEL))],
        out_specs=row(D_MODEL),
        out_shape=jax.ShapeDtypeStruct((n, D_MODEL), F32),
        compiler_params=_params(("parallel",)),
        name="mla_post",
    )(o, g, x, w_out, final_g)


def _rotate_half_cols(w):
    half = QK_ROPE // 2
    return jnp.concatenate([-w[..., half:], w[..., :half]], axis=-1)


def _prep_rwkv(j, rw_in, rw_w0, rw_w2, rw_a0, rw_a2, rw_kk, rw_ka, rw_rk, rw_lnx_g, rw_lnx_b,
               rw_out):
    e = D_INNER
    w = rw_in[j]
    w_main = w[:, :4 * e].astype(BF16)
    ww = w[:, 4 * e:4 * e + 2 * LORA]
    wa = w[:, 4 * e + 2 * LORA:]
    z = jnp.zeros_like(ww)
    w_small = jnp.concatenate([jnp.concatenate([ww, z], axis=1),
                               jnp.concatenate([z, wa], axis=1)], axis=0).astype(BF16)
    return dict(
        w_main=w_main, w_small=w_small, w0=rw_w0[j], w2=rw_w2[j].astype(BF16), a0=rw_a0[j],
        a2=rw_a2[j].astype(BF16), k_k=rw_kk[j][None], k_a=rw_ka[j][None],
        r_k=rw_rk[j].reshape(1, e), lnx_g=rw_lnx_g[j][None], lnx_b=rw_lnx_b[j][None],
        w_out=rw_out[j].astype(BF16))


def _prep_mla(j, ml_in, ml_qn, ml_kvn, ml_uq, ml_ukv, ml_out):
    w = ml_in[j]
    lo = Q_LORA + KV_LORA
    w_kpe = w[:, lo:lo + QK_ROPE]
    w_small = jnp.concatenate([w[:, :lo], w_kpe, _rotate_half_cols(w_kpe)], axis=1).astype(BF16)
    w_g = w[:, lo + QK_ROPE:].astype(BF16)
    uq = ml_uq[j].reshape(Q_LORA, MLA_HEADS, QK_HEAD)
    pe = uq[..., QK_NOPE:]
    wq = jnp.concatenate([uq[..., :QK_NOPE], pe, _rotate_half_cols(pe)], axis=-1)
    wq = wq.reshape(Q_LORA, MLA_HEADS * Q_PAD).astype(BF16)
    ukv = ml_ukv[j].reshape(KV_LORA, MLA_HEADS, QK_NOPE + V_HEAD)
    wkv = jnp.concatenate([ukv[..., :QK_NOPE].reshape(KV_LORA, -1),
                           ukv[..., QK_NOPE:].reshape(KV_LORA, -1)], axis=1).astype(BF16)
    return dict(w_small=w_small, w_g=w_g, qn=ml_qn[j][None], kvn=ml_kvn[j][None], wq=wq,
                wkv=wkv, w_out=ml_out[j].astype(BF16))


def _rope_tables(seq):
    inv_freq = 1.0 / (ROPE_THETA ** (jnp.arange(0, QK_ROPE, 2, dtype=F32) / QK_ROPE))
    ang = jnp.arange(seq, dtype=F32)[:, None] * inv_freq[None, :]
    z = jnp.zeros((seq, LANES - QK_ROPE), F32)
    cs = jnp.concatenate([jnp.cos(ang), jnp.cos(ang), z], axis=1)
    sn = jnp.concatenate([jnp.sin(ang), jnp.sin(ang), z], axis=1)
    return cs, sn


def _trunk(x3, ln_g, final_g, rw, ml):
    batch, seq, _ = x3.shape
    x = x3.reshape(batch * seq, D_MODEL)
    cs, sn = _rope_tables(seq)
    for i in range(DEPTH):
        g = ln_g[i][None]
        if i % 2 == 0:
            p = rw[i // 2]
            xs4, xs2 = _rw_prep(x, g, p["mu"], seq)
            proj = _matmul_streams(xs4, p["w_main"])
            small = _matmul(xs2, p["w_small"])
            yf, yb = _wkv(proj, small, p["w0"], p["w2"], p["a0"], p["a2"], p["k_k"], p["k_a"],
                          batch, seq)
            x = _rw_post(yf, yb, proj, small, x, p["a0"], p["a2"], p["k_a"], p["r_k"],
                         p["lnx_g"], p["lnx_b"], p["w_out"])
        else:
            p = ml[i // 2]
            small = _norm_matmul(x, g, p["w_small"], tn=SMALL_COLS)
            gate = _norm_matmul(x, g, p["w_g"], out_dtype=BF16)
            q, k, v = _ml_up(small, cs, sn, p["qn"], p["kvn"], p["wq"], p["wkv"], seq)
            o = _attention(q, k, v, batch, seq)
            x = _ml_post(o, gate, x, p["w_out"], final_g[None], final=(i == DEPTH - 1))
    return x.reshape(batch, seq, D_MODEL)


def kernel(x_prompt, x_sample, ln_g, final_g, rw_mu, rw_in, rw_w0, rw_w2, rw_a0, rw_a2, rw_kk,
           rw_ka, rw_rk, rw_lnx_g, rw_lnx_b, rw_out, ml_in, ml_qn, ml_kvn, ml_uq, ml_ukv, ml_out):
    rw = []
    for j in range(rw_in.shape[0]):
        p = _prep_rwkv(j, rw_in, rw_w0, rw_w2, rw_a0, rw_a2, rw_kk, rw_ka, rw_rk, rw_lnx_g,
                       rw_lnx_b, rw_out)
        p["mu"] = rw_mu[j]
        rw.append(p)
    ml = [_prep_mla(j, ml_in, ml_qn, ml_kvn, ml_uq, ml_ukv, ml_out) for j in range(ml_in.shape[0])]
    y_prompt = _trunk(x_prompt, ln_g, final_g, rw, ml)
    y_sample = _trunk(x_sample, ln_g, final_g, rw, ml)
    return (y_prompt, y_sample)
```
